```python
import math
import jax, jax.numpy as jnp
from jax import lax
import numpy as np

D_MODEL = 4096
BATCH = 1
SEQ = 16384
DEPTH = 4

N_MIXERS = 2
N_ATTN_LAYERS = (DEPTH + 1) // 2
N_MLSTM_LAYERS = DEPTH // 2

DA_HEADS = 16
DA_HEAD_DIM = D_MODEL // (2 * DA_HEADS)
DA_Q_BLOCK = 128

ML_HEADS = 8
ML_QK_DIM = D_MODEL // (2 * ML_HEADS)
ML_V_DIM = D_MODEL // ML_HEADS
ML_QK_WIDTH = ML_HEADS * ML_QK_DIM
ML_CHUNK = 64
ML_CONV = 4
ML_FGATE_BIAS_LO = 3.0
ML_FGATE_BIAS_HI = 6.0

N_EXPERTS = 16
N_GROUPS = 4
EXPERTS_PER_GROUP = N_EXPERTS // N_GROUPS
TOP_K = 2
D_EXPERT = D_MODEL // 8

DN_ALPHA = (2 * DEPTH) ** 0.25
DN_BETA = (8 * DEPTH) ** -0.25

LN_EPS = 1e-5
RMS_EPS = 1e-6

kernel_name = "hybrid_diffattn_mlstm_shared_router_moe"


def layer_norm(x, g, b):
    xf = x.astype(jnp.float32)
    mu = jnp.mean(xf, axis=-1, keepdims=True)
    xc = xf - mu
    var = jnp.mean(xc * xc, axis=-1, keepdims=True)
    return (xc * lax.rsqrt(var + LN_EPS) * g.astype(jnp.float32) + b.astype(jnp.float32)).astype(x.dtype)


def head_rms(h):
    return h * lax.rsqrt(jnp.mean(h * h, axis=-1, keepdims=True) + RMS_EPS)


def diff_attention(x, w_in, lam_vecs, norm_g, w_out, lambda_init):
    B, S, _ = x.shape
    H, dh = DA_HEADS, DA_HEAD_DIM
    f32 = jnp.float32
    proj = x @ w_in
    q = proj[..., :D_MODEL].reshape(B, S, H, 2, dh).astype(f32) * (dh ** -0.5)
    k = proj[..., D_MODEL:2 * D_MODEL].reshape(B, S, H, 2, dh).astype(f32)
    v = proj[..., 2 * D_MODEL:].reshape(B, S, H, 2 * dh).astype(f32)
    lv = lam_vecs.astype(f32)
    lam = jnp.exp(jnp.sum(lv[0] * lv[1])) - jnp.exp(jnp.sum(lv[2] * lv[3])) + lambda_init
    nb = S // DA_Q_BLOCK
    q_blocks = jnp.moveaxis(q.reshape(B, nb, DA_Q_BLOCK, H, 2, dh), 1, 0)
    starts = jnp.arange(nb, dtype=jnp.int32) * DA_Q_BLOCK
    k_pos = jnp.arange(S, dtype=jnp.int32)

    def block(args):
        qb, start = args
        s = jnp.einsum('bqhcd,bkhcd->bhcqk', qb, k)
        q_pos = start + jnp.arange(DA_Q_BLOCK, dtype=jnp.int32)
        mask = k_pos[None, :] <= q_pos[:, None]
        p = jax.nn.softmax(jnp.where(mask, s, -jnp.inf), axis=-1)
        a = p[:, :, 0] - lam * p[:, :, 1]
        return jnp.einsum('bhqk,bkhe->bqhe', a, v)

    o = lax.map(block, (q_blocks, starts))
    o = jnp.moveaxis(o, 0, 1).reshape(B, S, H, 2 * dh)
    o = head_rms(o) * norm_g.astype(f32) * (1.0 - lambda_init)
    return o.reshape(B, S, D_MODEL).astype(x.dtype) @ w_out


def causal_conv(u, w):
    C = u.shape[-1]
    return lax.conv_general_dilated(u, w[:, None, :].astype(u.dtype), window_strides=(1,),
                                    padding=[(ML_CONV - 1, 0)],
                                    dimension_numbers=('NWC', 'WIO', 'NWC'),
                                    feature_group_count=C)


def mlstm(x, w_in, gate_b, conv_w, norm_g, w_out):
    B, S, _ = x.shape
    H, dk, dv, L = ML_HEADS, ML_QK_DIM, ML_V_DIM, ML_CHUNK
    QK = ML_QK_WIDTH
    f32 = jnp.float32
    proj = x @ w_in
    qk = jax.nn.silu(causal_conv(proj[..., :2 * QK], conv_w))
    v = proj[..., 2 * QK:2 * QK + D_MODEL]
    o = proj[..., 2 * QK + D_MODEL:2 * QK + 2 * D_MODEL]
    gates = proj[..., 2 * QK + 2 * D_MODEL:].astype(f32) + gate_b.astype(f32)
    q = qk[..., :QK].reshape(B, S, H, dk).astype(f32) * (dk ** -0.5)
    k = qk[..., QK:].reshape(B, S, H, dk).astype(f32)
    v = v.reshape(B, S, H, dv).astype(f32)
    ig = gates[..., :H]
    lf = jax.nn.log_sigmoid(gates[..., H:])
    nc = S // L

    def to_chunks(t):
        return t.reshape(B, nc, L, H, -1).transpose(1, 0, 3, 2, 4)

    def gate_chunks(t):
        return t.reshape(B, nc, L, H).transpose(1, 0, 3, 2)

    causal = jnp.tril(jnp.ones((L, L), dtype=bool))

    def step(carry, inp):
        C, n, m = carry
        qc, kc, vc, igc, lfc = inp
        b = jnp.cumsum(lfc, axis=-1)
        dlog = jnp.where(causal, b[..., :, None] - b[..., None, :] + igc[..., None, :], -jnp.inf)
        inter = b + m[..., None]
        m_out = jnp.maximum(inter, jnp.max(dlog, axis=-1))
        dw = jnp.exp(dlog - m_out[..., None])
        inter_w = jnp.exp(inter - m_out)
        scores = jnp.einsum('bhjd,bhsd->bhjs', qc, kc) * dw
        num = inter_w[..., None] * jnp.einsum('bhvd,bhjd->bhjv', C, qc) + jnp.einsum('bhjs,bhsv->bhjv', scores, vc)
        den = inter_w * jnp.einsum('bhd,bhjd->bhj', n, qc) + jnp.sum(scores, axis=-1)
        h = num / jnp.maximum(jnp.abs(den), jnp.exp(-m_out))[..., None]
        b_last = b[..., -1]
        g = b_last[..., None] - b + igc
        m_new = jnp.maximum(b_last + m, jnp.max(g, axis=-1))
        decay = jnp.exp(b_last + m - m_new)
        wg = jnp.exp(g - m_new[..., None])
        C = decay[..., None, None] * C + jnp.einsum('bhs,bhsv,bhsd->bhvd', wg, vc, kc)
        n = decay[..., None] * n + jnp.einsum('bhs,bhsd->bhd', wg, kc)
        return (C, n, m_new), h

    init = (jnp.zeros((B, H, dv, dk), f32), jnp.zeros((B, H, dk), f32), jnp.zeros((B, H), f32))
    _, h = lax.scan(step, init, (to_chunks(q), to_chunks(k), to_chunks(v), gate_chunks(ig), gate_chunks(lf)))
    h = head_rms(h.transpose(1, 0, 3, 2, 4).reshape(B, S, H, dv))
    h = h.reshape(B, S, D_MODEL) * norm_g.astype(f32) * jax.nn.sigmoid(o.astype(f32))
    return h.astype(x.dtype) @ w_out


def moe(x, w_router, b_router, w_gate, w_up, w_down):
    B, S, _ = x.shape
    f32 = jnp.float32
    logits = jnp.einsum('bsd,de->bse', x.astype(f32), w_router.astype(f32))
    probs = jax.nn.softmax(logits, axis=-1)
    sel = (probs + b_router.astype(f32)).reshape(B, S, N_GROUPS, EXPERTS_PER_GROUP)
    group_score = jnp.sum(lax.top_k(sel, TOP_K)[0], axis=-1)
    grp = jnp.argmax(group_score, axis=-1)
    in_group = jnp.take_along_axis(sel, grp[..., None, None], axis=2)[..., 0, :]
    _, local = lax.top_k(in_group, TOP_K)
    idx = grp[..., None] * EXPERTS_PER_GROUP + local
    w = jnp.take_along_axis(probs, idx, axis=-1)
    w = w / jnp.sum(w, axis=-1, keepdims=True)
    gates = jnp.sum(jax.nn.one_hot(idx, N_EXPERTS, dtype=f32) * w[..., None], axis=-2).astype(x.dtype)
    y = jnp.zeros_like(x)
    for e in range(N_EXPERTS):
        h = jax.nn.silu(x @ w_gate[e]) * (x @ w_up[e])
        y = y + gates[..., e:e + 1] * (h @ w_down[e])
    return y


def setup_inputs(seed: int = 0) -> dict:
    key = jax.random.key(seed)
    ks = jax.random.split(key, 24)
    f32 = jnp.float32
    D, NA, NM, H = D_MODEL, N_ATTN_LAYERS, N_MLSTM_LAYERS, ML_HEADS
    s_in = D ** -0.5

    def nrm(k, shape, scale):
        return jax.random.normal(k, shape, f32) * scale

    x = nrm(ks[0], (BATCH, SEQ, D), 1.0)
    attn_w_in = jnp.concatenate([nrm(ks[1], (NA, D, 2 * D), s_in),
                                 nrm(ks[2], (NA, D, D), s_in * DN_BETA)], axis=-1)
    attn_lambda = nrm(ks[3], (NA, 4, DA_HEAD_DIM), 0.1)
    attn_norm_g = 1.0 + nrm(ks[4], (NA, 2 * DA_HEAD_DIM), 0.02)
    attn_w_out = nrm(ks[5], (NA, D, D), s_in * DN_BETA)
    mlstm_w_in = jnp.concatenate([nrm(ks[6], (NM, D, 2 * ML_QK_WIDTH), s_in),
                                  nrm(ks[7], (NM, D, D), s_in * DN_BETA),
                                  nrm(ks[8], (NM, D, D + 2 * H), s_in)], axis=-1)
    f_bias = jnp.broadcast_to(jnp.linspace(ML_FGATE_BIAS_LO, ML_FGATE_BIAS_HI, H, dtype=f32), (NM, H))
    mlstm_gate_b = jnp.concatenate([nrm(ks[9], (NM, H), 0.1),
                                    f_bias + nrm(ks[10], (NM, H), 0.1)], axis=-1)
    mlstm_conv_w = nrm(ks[11], (NM, ML_CONV, 2 * ML_QK_WIDTH), ML_CONV ** -0.5)
    mlstm_norm_g = 1.0 + nrm(ks[12], (NM, D), 0.02)
    mlstm_w_out = nrm(ks[13], (NM, D, D), s_in * DN_BETA)
    ln_mix_g = 1.0 + nrm(ks[14], (DEPTH, D), 0.02)
    ln_mix_b = nrm(ks[15], (DEPTH, D), 0.02)
    ln_ffn_g = 1.0 + nrm(ks[16], (DEPTH, D), 0.02)
    ln_ffn_b = nrm(ks[17], (DEPTH, D), 0.02)
    w_router = nrm(ks[18], (D, N_EXPERTS), s_in)
    b_router = nrm(ks[19], (N_EXPERTS,), 0.01)
    moe_w_gate = nrm(ks[20], (DEPTH, N_EXPERTS, D, D_EXPERT), s_in)
    moe_w_up = nrm(ks[21], (DEPTH, N_EXPERTS, D, D_EXPERT), s_in)
    moe_w_down = nrm(ks[22], (DEPTH, N_EXPERTS, D_EXPERT, D), (D_EXPERT ** -0.5) * DN_BETA)
    return {"x": x, "attn_w_in": attn_w_in, "attn_lambda": attn_lambda, "attn_norm_g": attn_norm_g,
            "attn_w_out": attn_w_out, "mlstm_w_in": mlstm_w_in, "mlstm_gate_b": mlstm_gate_b,
            "mlstm_conv_w": mlstm_conv_w, "mlstm_norm_g": mlstm_norm_g, "mlstm_w_out": mlstm_w_out,
            "ln_mix_g": ln_mix_g, "ln_mix_b": ln_mix_b, "ln_ffn_g": ln_ffn_g, "ln_ffn_b": ln_ffn_b,
            "w_router": w_router, "b_router": b_router, "moe_w_gate": moe_w_gate,
            "moe_w_up": moe_w_up, "moe_w_down": moe_w_down}


def reference(x, attn_w_in, attn_lambda, attn_norm_g, attn_w_out, mlstm_w_in, mlstm_gate_b,
              mlstm_conv_w, mlstm_norm_g, mlstm_w_out, ln_mix_g, ln_mix_b, ln_ffn_g, ln_ffn_b,
              w_router, b_router, moe_w_gate, moe_w_up, moe_w_down):
    for i in range(DEPTH):
        j = i // N_MIXERS
        if i % N_MIXERS == 0:
            lambda_init = 0.8 - 0.6 * math.exp(-0.3 * i)
            mix = diff_attention(x, attn_w_in[j], attn_lambda[j], attn_norm_g[j], attn_w_out[j], lambda_init)
        else:
            mix = mlstm(x, mlstm_w_in[j], mlstm_gate_b[j], mlstm_conv_w[j], mlstm_norm_g[j], mlstm_w_out[j])
        x = layer_norm(DN_ALPHA * x + mix, ln_mix_g[i], ln_mix_b[i])
        ffn = moe(x, w_router, b_router, moe_w_gate[i], moe_w_up[i], moe_w_down[i])
        x = layer_norm(DN_ALPHA * x + ffn, ln_ffn_g[i], ln_ffn_b[i])
    return x
```

```python
import functools
import math

import jax
import jax.numpy as jnp
from jax import lax
from jax.experimental import pallas as pl
from jax.experimental.pallas import tpu as pltpu

F32 = jnp.float32
BF16 = jnp.bfloat16

LN_EPS = 1e-5
RMS_EPS = 1e-6
N_GROUPS = 4
TOP_K = 2
LOG2E = math.log2(math.e)

V7X_VMEM_BYTES = 64 * 1024 * 1024
VMEM_LIMIT = V7X_VMEM_BYTES - 8 * 1024 * 1024
LANE = 128


def _cparams(sem):
    return pltpu.CompilerParams(dimension_semantics=sem, vmem_limit_bytes=VMEM_LIMIT)


def _tile(n, pref):
    t = min(n, pref)
    while n % t:
        t //= 2
    return t


def _mm_kernel(a_ref, w_ref, o_ref):
    a = a_ref[...].astype(BF16)
    w = w_ref[...].astype(BF16)
    o_ref[...] = jnp.dot(a, w, preferred_element_type=F32).astype(o_ref.dtype)


def matmul(a, w3, layer, col0, n, out_dtype, tm=1024, tn=512):
    m, k = a.shape
    tm = _tile(m, tm)
    tn = _tile(n, tn)
    assert col0 % tn == 0
    cb = col0 // tn
    return pl.pallas_call(
        _mm_kernel,
        grid=(m // tm, n // tn),
        in_specs=[pl.BlockSpec((tm, k), lambda i, j: (i, 0)),
                  pl.BlockSpec((None, k, tn), lambda i, j: (layer, 0, cb + j))],
        out_specs=pl.BlockSpec((tm, tn), lambda i, j: (i, j)),
        out_shape=jax.ShapeDtypeStruct((m, n), out_dtype),
        compiler_params=_cparams(("parallel", "parallel")),
        name="proj_matmul",
    )(a, w3)


def _layer_norm_rows(z, g, b):
    mu = jnp.mean(z, axis=-1, keepdims=True)
    zc = z - mu
    var = jnp.mean(zc * zc, axis=-1, keepdims=True)
    return zc * lax.rsqrt(var + LN_EPS) * g + b


def _ln_kernel(x_ref, y_ref, g_ref, b_ref, of_ref, ob_ref, *, alpha):
    z = alpha * x_ref[...] + y_ref[...].astype(F32)
    out = _layer_norm_rows(z, g_ref[...], b_ref[...])
    of_ref[...] = out
    ob_ref[...] = out.astype(BF16)


def residual_layer_norm(x, y, g, b, alpha, tr=256):
    s, d = x.shape
    tr = _tile(s, tr)
    row = pl.BlockSpec((tr, d), lambda i: (i, 0))
    vec = pl.BlockSpec((1, d), lambda i: (0, 0))
    return pl.pallas_call(
        functools.partial(_ln_kernel, alpha=alpha),
        grid=(s // tr,),
        in_specs=[row, row, vec, vec],
        out_specs=[row, row],
        out_shape=[jax.ShapeDtypeStruct((s, d), F32), jax.ShapeDtypeStruct((s, d), BF16)],
        compiler_params=_cparams(("parallel",)),
        name="residual_layer_norm",
    )(x, y, g.reshape(1, d), b.reshape(1, d))


def _attn_kernel(lam_ref, q_ref, k_ref, v_ref, g_ref, o_ref, m_sc, l_sc, acc_sc, *, t, dh, q_scale, out_scale):
    i = pl.program_id(1)
    qb = (q_ref[...].astype(F32) * q_scale).astype(BF16)
    m_sc[...] = jnp.full(m_sc.shape, -jnp.inf, F32)
    l_sc[...] = jnp.zeros(l_sc.shape, F32)
    acc_sc[...] = jnp.zeros(acc_sc.shape, F32)

    def block(j, masked):
        off = pl.multiple_of(j * t, t)
        kb = k_ref[pl.ds(off, t), :]
        vb = v_ref[pl.ds(off, t), :]
        for c in range(2):
            s = lax.dot_general(qb[:, c * dh:(c + 1) * dh], kb[:, c * dh:(c + 1) * dh],
                                (((1,), (1,)), ((), ())), preferred_element_type=F32)
            if masked:
                row = lax.broadcasted_iota(jnp.int32, (t, t), 0)
                col = lax.broadcasted_iota(jnp.int32, (t, t), 1)
                s = jnp.where(col <= row, s, -jnp.inf)
            m_prev = m_sc[c]
            m_new = jnp.maximum(m_prev, jnp.max(s, axis=-1, keepdims=True))
            a = jnp.exp2(m_prev - m_new)
            p = jnp.exp2(s - m_new)
            l_sc[c] = a * l_sc[c] + jnp.sum(p, axis=-1, keepdims=True)
            acc_sc[c] = a * acc_sc[c] + jnp.dot(p.astype(BF16), vb, preferred_element_type=F32)
            m_sc[c] = m_new

    def body(j, carry):
        block(j, False)
        return carry

    lax.fori_loop(0, i, body, 0)
    block(i, True)

    lam = lam_ref[0, 0]
    o = acc_sc[0] / l_sc[0] - lam * (acc_sc[1] / l_sc[1])
    o = o * lax.rsqrt(jnp.mean(o * o, axis=-1, keepdims=True) + RMS_EPS)
    o_ref[...] = (o * g_ref[...] * out_scale).astype(o_ref.dtype)


def diff_attention_core(proj, lam, norm_g, lambda_init, n_heads, dh, t=512):
    s, d3 = proj.shape
    hd = 2 * dh
    d = n_heads * hd
    t = _tile(s, t)
    kern = functools.partial(_attn_kernel, t=t, dh=dh, q_scale=(dh ** -0.5) * LOG2E,
                             out_scale=1.0 - lambda_init)
    return pl.pallas_call(
        kern,
        grid=(n_heads, s // t),
        in_specs=[pl.BlockSpec(memory_space=pltpu.SMEM),
                  pl.BlockSpec((t, hd), lambda h, i: (i, h)),
                  pl.BlockSpec((s, hd), lambda h, i: (0, n_heads + h)),
                  pl.BlockSpec((s, hd), lambda h, i: (0, 2 * n_heads + h)),
                  pl.BlockSpec((1, hd), lambda h, i: (0, 0))],
        out_specs=pl.BlockSpec((t, hd), lambda h, i: (i, h)),
        out_shape=jax.ShapeDtypeStruct((s, d), BF16),
        scratch_shapes=[pltpu.VMEM((2, t, 1), F32), pltpu.VMEM((2, t, 1), F32), pltpu.VMEM((2, t, hd), F32)],
        compiler_params=_cparams(("parallel", "arbitrary")),
        name="diff_attention",
    )(lam.reshape(1, 1), proj, proj, proj, norm_g.reshape(1, hd))


def _gates_kernel(a_ref, w_ref, b_ref, o_ref, *, n_heads):
    z = jnp.dot(a_ref[...], w_ref[...].astype(BF16), preferred_element_type=F32) + b_ref[...]
    log_sig = jnp.minimum(z, 0.0) - jnp.log(1.0 + jnp.exp(-jnp.abs(z)))
    col = lax.broadcasted_iota(jnp.int32, z.shape, 1)
    o_ref[...] = jnp.where(col >= n_heads, log_sig, z)


def mlstm_gates(a, w_g, bias, n_heads, tm=1024):
    m, k = a.shape
    n = w_g.shape[1]
    tm = _tile(m, tm)
    return pl.pallas_call(
        functools.partial(_gates_kernel, n_heads=n_heads),
        grid=(m // tm,),
        in_specs=[pl.BlockSpec((tm, k), lambda i: (i, 0)),
                  pl.BlockSpec((k, n), lambda i: (0, 0)),
                  pl.BlockSpec((1, n), lambda i: (0, 0))],
        out_specs=pl.BlockSpec((tm, n), lambda i: (i, 0)),
        out_shape=jax.ShapeDtypeStruct((m, n), F32),
        compiler_params=_cparams(("parallel",)),
        name="mlstm_gates",
    )(a, w_g, bias.reshape(1, n))


def _conv_kernel(u_ref, w_ref, o_ref, tail_sc, *, n_taps, q_tiles, q_scale):
    j = pl.program_id(0)
    ti = pl.program_id(1)

    @pl.when(ti == 0)
    def _():
        tail_sc[...] = jnp.zeros(tail_sc.shape, F32)

    u = u_ref[...]
    tr = u.shape[0]
    tail = tail_sc[...]
    row8 = lax.broadcasted_iota(jnp.int32, tail.shape, 0)
    acc = u * w_ref[n_taps - 1:n_taps, :]
    for sh in range(1, n_taps):
        rolled = pltpu.roll(u, sh, 0)
        head = jnp.where(row8 < sh, pltpu.roll(tail, sh, 0), rolled[:8])
        shifted = jnp.concatenate([head, rolled[8:]], axis=0)
        acc = acc + shifted * w_ref[n_taps - 1 - sh:n_taps - sh, :]
    tail_sc[...] = u[tr - 8:, :]
    y = acc * jax.nn.sigmoid(acc)
    y = y * jnp.where(j < q_tiles, q_scale, 1.0)
    o_ref[...] = y.astype(o_ref.dtype)


def causal_conv_silu(u, w, qk_width, q_scale, tr=512, tc=512):
    s, c = u.shape
    n_taps = w.shape[0]
    tr = _tile(s, tr)
    tc = _tile(qk_width, tc)
    kern = functools.partial(_conv_kernel, n_taps=n_taps, q_tiles=qk_width // tc, q_scale=q_scale)
    return pl.pallas_call(
        kern,
        grid=(c // tc, s // tr),
        in_specs=[pl.BlockSpec((tr, tc), lambda j, i: (i, j)),
                  pl.BlockSpec((n_taps, tc), lambda j, i: (0, j))],
        out_specs=pl.BlockSpec((tr, tc), lambda j, i: (i, j)),
        out_shape=jax.ShapeDtypeStruct((s, c), BF16),
        scratch_shapes=[pltpu.VMEM((8, tc), F32)],
        compiler_params=_cparams(("parallel", "arbitrary")),
        name="causal_conv_silu",
    )(u, w)


def _mlstm_kernel(q_ref, k_ref, v_ref, og_ref, gt_ref, ng_ref, o_ref, c_sc, m_sc, *, lc, dv, n_heads):
    h = pl.program_id(0)
    ci = pl.program_id(1)

    @pl.when(ci == 0)
    def _():
        c_sc[...] = jnp.zeros(c_sc.shape, F32)
        m_sc[...] = jnp.zeros(m_sc.shape, F32)

    ig_row = gt_ref[pl.ds(h, 1), :]
    lf_row = gt_ref[pl.ds(h + n_heads, 1), :]

    lane8 = lax.broadcasted_iota(jnp.int32, (8, lc), 1)
    b8 = jnp.broadcast_to(lf_row, (8, lc))
    d = 1
    while d < lc:
        b8 = b8 + jnp.where(lane8 >= d, pltpu.roll(b8, d, 1), 0.0)
        d *= 2
    b_row = b8[0:1, :]

    half = lc // 2
    rows = lax.broadcasted_iota(jnp.int32, (lc, lc), 0)
    cols = lax.broadcasted_iota(jnp.int32, (lc, lc), 1)
    stacked = jnp.where(rows < half, jnp.broadcast_to(b_row, (lc, lc)), jnp.broadcast_to(ig_row, (lc, lc)))
    stacked_t = stacked.T
    b_col = stacked_t[:, 0:1]
    ig_col = stacked_t[:, half:half + 1]

    m_prev = m_sc[...]
    dlog = jnp.where(cols <= rows, b_col - b_row + ig_row, -jnp.inf)
    inter = b_col + m_prev
    m_out = jnp.maximum(inter, jnp.max(dlog, axis=-1, keepdims=True))
    dw = jnp.exp(dlog - m_out)
    inter_w = jnp.exp(inter - m_out)

    q = q_ref[...]
    k = k_ref[...]
    ones_col = (lax.broadcasted_iota(jnp.int32, (lc, LANE), 1) == 0).astype(BF16)
    v_ext = jnp.concatenate([v_ref[...], ones_col], axis=1)

    scores = lax.dot_general(q, k, (((1,), (1,)), ((), ())), preferred_element_type=F32) * dw
    c_ext = c_sc[...]
    num_ext = inter_w * jnp.dot(q, c_ext.astype(BF16), preferred_element_type=F32) \
        + jnp.dot(scores.astype(BF16), v_ext, preferred_element_type=F32)
    num = num_ext[:, :dv]
    den = num_ext[:, dv:dv + 1]
    hid = num / jnp.maximum(jnp.abs(den), jnp.exp(-m_out))

    b_last = b_row[:, lc - 1:lc]
    g_col = b_last - b_col + ig_col
    m_new = jnp.maximum(b_last + m_prev, jnp.max(g_col, axis=0, keepdims=True))
    decay = jnp.exp(b_last + m_prev - m_new)
    wg_col = jnp.exp(g_col - m_new)
    vw_ext = (v_ext.astype(F32) * wg_col).astype(BF16)
    k_t = k.astype(F32).T.astype(BF16)
    c_sc[...] = decay * c_ext + jnp.dot(k_t, vw_ext, preferred_element_type=F32)
    m_sc[...] = m_new

    hn = hid * lax.rsqrt(jnp.mean(hid * hid, axis=-1, keepdims=True) + RMS_EPS)
    o_ref[...] = (hn * ng_ref[...] * jax.nn.sigmoid(og_ref[...].astype(F32))).astype(o_ref.dtype)


def mlstm_core(qk, vo, gates_t, norm_g, n_heads, dk, dv, lc=256):
    s = qk.shape[0]
    d = n_heads * dv
    lc = _tile(s, lc)
    kern = functools.partial(_mlstm_kernel, lc=lc, dv=dv, n_heads=n_heads)
    return pl.pallas_call(
        kern,
        grid=(n_heads, s // lc),
        in_specs=[pl.BlockSpec((lc, dk), lambda h, c: (c, h)),
                  pl.BlockSpec((lc, dk), lambda h, c: (c, n_heads + h)),
                  pl.BlockSpec((lc, dv), lambda h, c: (c, h)),
                  pl.BlockSpec((lc, dv), lambda h, c: (c, n_heads + h)),
                  pl.BlockSpec((2 * n_heads, lc), lambda h, c: (0, c)),
                  pl.BlockSpec((1, dv), lambda h, c: (0, h))],
        out_specs=pl.BlockSpec((lc, dv), lambda h, c: (c, h)),
        out_shape=jax.ShapeDtypeStruct((s, d), BF16),
        scratch_shapes=[pltpu.VMEM((dk, dv + LANE), F32), pltpu.VMEM((1, 1), F32)],
        compiler_params=_cparams(("parallel", "arbitrary")),
        name="mlstm_chunks",
    )(qk, qk, vo, vo, gates_t, norm_g.reshape(1, d))


def _split_bf16(x):
    hi = x.astype(BF16)
    lo = (x - hi.astype(F32)).astype(BF16)
    return hi, lo


def _router_kernel(x_ref, wt_ref, br_ref, ri_ref, rw_ref, cnt_ref, lg_sc, base_sc, *, n_exp, tm):
    step = pl.program_id(0)
    per_group = n_exp // N_GROUPS

    @pl.when(step == 0)
    def _():
        base_sc[...] = jnp.zeros(base_sc.shape, F32)

    nt = (((1,), (1,)), ((), ()))
    x_hi, x_lo = _split_bf16(x_ref[...])
    w_hi, w_lo = _split_bf16(wt_ref[...])
    lg_sc[...] = (lax.dot_general(w_hi, x_hi, nt, preferred_element_type=F32)
                  + lax.dot_general(w_hi, x_lo, nt, preferred_element_type=F32)
                  + lax.dot_general(w_lo, x_hi, nt, preferred_element_type=F32))

    logit = [lg_sc[e:e + 1, :] for e in range(n_exp)]
    mx = functools.reduce(jnp.maximum, logit)
    ex = [jnp.exp(l - mx) for l in logit]
    inv = 1.0 / functools.reduce(lambda a, b: a + b, ex)
    prob = [e * inv for e in ex]
    sel = [prob[e] + br_ref[e:e + 1, :] for e in range(n_exp)]

    def top2_sum(a):
        hi01, lo01 = jnp.maximum(a[0], a[1]), jnp.minimum(a[0], a[1])
        hi23, lo23 = jnp.maximum(a[2], a[3]), jnp.minimum(a[2], a[3])
        return jnp.maximum(hi01, hi23) + jnp.maximum(jnp.minimum(hi01, hi23), jnp.maximum(lo01, lo23))

    assert per_group == 4
    gscore = [top2_sum(sel[g * per_group:(g + 1) * per_group]) for g in range(N_GROUPS)]
    best, grp = gscore[0], jnp.zeros_like(gscore[0], dtype=jnp.int32)
    for g in range(1, N_GROUPS):
        upd = gscore[g] > best
        grp = jnp.where(upd, g, grp)
        best = jnp.where(upd, gscore[g], best)

    def pick(vals, i):
        out = vals[i]
        for g in range(1, N_GROUPS):
            out = jnp.where(grp == g, vals[g * per_group + i], out)
        return out

    gs = [pick(sel, i) for i in range(per_group)]
    gp = [pick(prob, i) for i in range(per_group)]
    v1, i1, p1 = gs[0], jnp.zeros_like(grp), gp[0]
    for i in range(1, per_group):
        upd = gs[i] > v1
        v1, i1, p1 = jnp.where(upd, gs[i], v1), jnp.where(upd, i, i1), jnp.where(upd, gp[i], p1)
    v2 = jnp.full_like(v1, -jnp.inf)
    i2, p2 = jnp.zeros_like(grp), jnp.zeros_like(p1)
    for i in range(per_group):
        upd = (i1 != i) & (gs[i] > v2)
        v2, i2, p2 = jnp.where(upd, gs[i], v2), jnp.where(upd, i, i2), jnp.where(upd, gp[i], p2)
    e1 = grp * per_group + i1
    e2 = grp * per_group + i2
    wsum = p1 + p2
    rw_ref[...] = jnp.zeros(rw_ref.shape, F32)
    rw_ref[0:1, :] = p1 / wsum
    rw_ref[1:2, :] = p2 / wsum

    for e in range(n_exp):
        lg_sc[e:e + 1, :] = ((e1 == e) | (e2 == e)).astype(F32)
    mask = lg_sc[...]
    r_i = lax.broadcasted_iota(jnp.int32, (tm, tm), 0)
    c_i = lax.broadcasted_iota(jnp.int32, (tm, tm), 1)
    upper = (r_i < c_i).astype(BF16)
    lg_sc[...] = jnp.dot(mask.astype(BF16), upper, preferred_element_type=F32) + base_sc[...]
    new_base = base_sc[...] + jnp.sum(mask, axis=1, keepdims=True)
    base_sc[...] = new_base
    cnt_ref[...] = jnp.broadcast_to(new_base, cnt_ref.shape).astype(jnp.int32)
    r1 = jnp.zeros_like(p1)
    r2 = jnp.zeros_like(p1)
    for e in range(n_exp):
        excl_e = lg_sc[e:e + 1, :]
        r1 = jnp.where(e1 == e, excl_e, r1)
        r2 = jnp.where(e2 == e, excl_e, r2)
    ri_ref[...] = jnp.zeros(ri_ref.shape, jnp.int32)
    ri_ref[0:1, :] = e1
    ri_ref[1:2, :] = e2
    ri_ref[2:3, :] = r1.astype(jnp.int32)
    ri_ref[3:4, :] = r2.astype(jnp.int32)


def router(x, w_router, b_router, tm=512):
    s, d = x.shape
    n_exp = w_router.shape[1]
    tm = _tile(s, tm)
    kern = functools.partial(_router_kernel, n_exp=n_exp, tm=tm)
    return pl.pallas_call(
        kern,
        grid=(s // tm,),
        in_specs=[pl.BlockSpec((tm, d), lambda i: (i, 0)),
                  pl.BlockSpec((n_exp, d), lambda i: (0, 0)),
                  pl.BlockSpec((n_exp, 1), lambda i: (0, 0))],
        out_specs=[pl.BlockSpec((8, tm), lambda i: (0, i)),
                   pl.BlockSpec((8, tm), lambda i: (0, i)),
                   pl.BlockSpec((n_exp, LANE), lambda i: (0, 0))],
        out_shape=[jax.ShapeDtypeStruct((8, s), jnp.int32), jax.ShapeDtypeStruct((8, s), F32),
                   jax.ShapeDtypeStruct((n_exp, LANE), jnp.int32)],
        scratch_shapes=[pltpu.VMEM((n_exp, tm), F32), pltpu.VMEM((n_exp, 1), F32)],
        compiler_params=_cparams(("arbitrary",)),
        name="moe_router",
    )(x, w_router.T, b_router.reshape(n_exp, 1))


def _dispatch_kernel(dest_ref, pad_ref, x_ref, z_ref, xs_ref, sem, *, n_tok, n_exp):
    def row_copy(src, r_src, r_dst):
        return pltpu.make_async_copy(src.at[pl.ds(r_src, 1), :], xs_ref.at[pl.ds(r_dst, 1), :], sem)

    def start_tok(t, c):
        row_copy(x_ref, t, dest_ref[t]).start()
        row_copy(x_ref, t, dest_ref[n_tok + t]).start()
        return c

    def wait_tok(t, c):
        row_copy(x_ref, t, dest_ref[t]).wait()
        row_copy(x_ref, t, dest_ref[n_tok + t]).wait()
        return c

    lax.fori_loop(0, n_tok, start_tok, 0)
    for e in range(n_exp):
        lo, hi = pad_ref[e], pad_ref[n_exp + e]
        lax.fori_loop(lo, hi, lambda r, c: (row_copy(z_ref, 0, r).start(), c)[1], 0)
    lax.fori_loop(0, n_tok, wait_tok, 0)
    for e in range(n_exp):
        lo, hi = pad_ref[e], pad_ref[n_exp + e]
        lax.fori_loop(lo, hi, lambda r, c: (row_copy(z_ref, 0, r).wait(), c)[1], 0)


def dispatch(x, dest, pad_bounds, n_slots, n_exp):
    s, d = x.shape
    kern = functools.partial(_dispatch_kernel, n_tok=s, n_exp=n_exp)
    return pl.pallas_call(
        kern,
        grid_spec=pltpu.PrefetchScalarGridSpec(
            num_scalar_prefetch=2,
            grid=(1,),
            in_specs=[pl.BlockSpec(memory_space=pl.ANY), pl.BlockSpec(memory_space=pl.ANY)],
            out_specs=pl.BlockSpec(memory_space=pl.ANY),
            scratch_shapes=[pltpu.SemaphoreType.DMA],
        ),
        out_shape=jax.ShapeDtypeStruct((n_slots, d), x.dtype),
        compiler_params=_cparams(("arbitrary",)),
        name="moe_dispatch",
    )(dest, pad_bounds, x, jnp.zeros((8, d), x.dtype))


def _expert_kernel(te_ref, ts_ref, nv_ref, xs_ref, wg_ref, wu_ref, wd_ref, ys_ref):
    @pl.when(pl.program_id(0) < nv_ref[0])
    def _():
        x = xs_ref[...].astype(BF16)
        g = jnp.dot(x, wg_ref[...], preferred_element_type=F32)
        u = jnp.dot(x, wu_ref[...], preferred_element_type=F32)
        hmid = (g * jax.nn.sigmoid(g) * u).astype(BF16)
        ys_ref[...] = jnp.dot(hmid, wd_ref[...], preferred_element_type=F32).astype(ys_ref.dtype)


def expert_mlp(xs, w_gate, w_up, w_down, tile_expert, tile_src, n_valid, tm):
    n_slots, d = xs.shape
    _, _, f = w_gate.shape
    n_tiles = n_slots // tm
    return pl.pallas_call(
        _expert_kernel,
        grid_spec=pltpu.PrefetchScalarGridSpec(
            num_scalar_prefetch=3,
            grid=(n_tiles,),
            in_specs=[pl.BlockSpec((tm, d), lambda t, te, ts, nv: (ts[t], 0)),
                      pl.BlockSpec((None, d, f), lambda t, te, ts, nv: (te[t], 0, 0)),
                      pl.BlockSpec((None, d, f), lambda t, te, ts, nv: (te[t], 0, 0)),
                      pl.BlockSpec((None, f, d), lambda t, te, ts, nv: (te[t], 0, 0))],
            out_specs=pl.BlockSpec((tm, d), lambda t, te, ts, nv: (ts[t], 0)),
        ),
        out_shape=jax.ShapeDtypeStruct((n_slots, d), F32),
        compiler_params=_cparams(("arbitrary",)),
        name="moe_experts",
    )(tile_expert, tile_src, n_valid, xs, w_gate, w_up, w_down)


def _combine_kernel(dest_ref, x_ref, w_ref, g_ref, b_ref, ys_ref, of_ref, ob_ref, buf, sem, *, tt, n_tok, alpha):
    i = pl.program_id(0)
    n_steps = pl.num_programs(0)

    def row_copy(step, slot, kk, r):
        tok = step * tt + r
        return pltpu.make_async_copy(ys_ref.at[pl.ds(dest_ref[kk * n_tok + tok], 1), :],
                                     buf.at[slot, kk, pl.ds(r, 1), :], sem.at[slot])

    def start_tile(step, slot):
        def body(r, c):
            row_copy(step, slot, 0, r).start()
            row_copy(step, slot, 1, r).start()
            return c
        lax.fori_loop(0, tt, body, 0)

    def wait_tile(step, slot):
        def body(r, c):
            row_copy(step, slot, 0, r).wait()
            row_copy(step, slot, 1, r).wait()
            return c
        lax.fori_loop(0, tt, body, 0)

    slot = i % 2

    @pl.when(i == 0)
    def _():
        start_tile(0, 0)

    @pl.when(i + 1 < n_steps)
    def _():
        start_tile(i + 1, 1 - slot)

    wait_tile(i, slot)
    w = w_ref[...]
    y = w[:, 0:1] * buf[slot, 0] + w[:, 1:2] * buf[slot, 1]
    out = _layer_norm_rows(alpha * x_ref[...] + y, g_ref[...], b_ref[...])
    of_ref[...] = out
    ob_ref[...] = out.astype(BF16)


def combine_layer_norm(x, ys, dest, w_tok, g, b, alpha, tt=256):
    s, d = x.shape
    tt = _tile(s, tt)
    kern = functools.partial(_combine_kernel, tt=tt, n_tok=s, alpha=alpha)
    row = lambda i, dr: (i, 0)
    fixed = lambda i, dr: (0, 0)
    return pl.pallas_call(
        kern,
        grid_spec=pltpu.PrefetchScalarGridSpec(
            num_scalar_prefetch=1,
            grid=(s // tt,),
            in_specs=[pl.BlockSpec((tt, d), row),
                      pl.BlockSpec((tt, TOP_K), row),
                      pl.BlockSpec((1, d), fixed),
                      pl.BlockSpec((1, d), fixed),
                      pl.BlockSpec(memory_space=pl.ANY)],
            out_specs=[pl.BlockSpec((tt, d), row), pl.BlockSpec((tt, d), row)],
            scratch_shapes=[pltpu.VMEM((2, TOP_K, tt, d), F32), pltpu.SemaphoreType.DMA((2,))],
        ),
        out_shape=[jax.ShapeDtypeStruct((s, d), F32), jax.ShapeDtypeStruct((s, d), BF16)],
        compiler_params=_cparams(("arbitrary",)),
        name="moe_combine_layer_norm",
    )(dest, x, w_tok, g.reshape(1, d), b.reshape(1, d), ys)


def moe_block(x_f32, w_router, b_router, w_gate, w_up, w_down, ln_g, ln_b, alpha, tm=256):
    s, d = x_f32.shape
    n_exp = w_router.shape[1]
    tm = _tile(s, tm)
    route_i, route_w, counts = router(x_f32, w_router, b_router)
    counts = counts[:, 0]
    padded = ((counts + tm - 1) // tm) * tm
    ends = jnp.cumsum(padded)
    offsets = ends - padded
    e1, e2, r1, r2 = route_i[0], route_i[1], route_i[2], route_i[3]
    dest = jnp.concatenate([offsets[e1] + r1, offsets[e2] + r2]).astype(jnp.int32)
    pad_bounds = jnp.concatenate([offsets + counts, ends]).astype(jnp.int32)
    n_slots = TOP_K * s + n_exp * tm
    n_tiles = n_slots // tm
    n_valid = (ends[-1] // tm).astype(jnp.int32)
    tile_start = jnp.arange(n_tiles, dtype=jnp.int32) * tm
    tile_expert = jnp.minimum(jnp.sum(tile_start[:, None] >= ends[None, :], axis=1), n_exp - 1).astype(jnp.int32)
    tile_src = jnp.minimum(jnp.arange(n_tiles, dtype=jnp.int32), n_valid - 1)
    tile_expert = tile_expert[tile_src]

    xs = dispatch(x_f32, dest, pad_bounds, n_slots, n_exp)
    ys = expert_mlp(xs, w_gate, w_up, w_down, tile_expert, tile_src, n_valid.reshape(1), tm)
    w_tok = jnp.stack([route_w[0], route_w[1]], axis=1)
    return combine_layer_norm(x_f32, ys, dest, w_tok, ln_g, ln_b, alpha)


def kernel(x, attn_w_in, attn_lambda, attn_norm_g, attn_w_out, mlstm_w_in, mlstm_gate_b, mlstm_conv_w,
           mlstm_norm_g, mlstm_w_out, ln_mix_g, ln_mix_b, ln_ffn_g, ln_ffn_b, w_router, b_router,
           moe_w_gate, moe_w_up, moe_w_down):
    batch, s, d = x.shape
    assert batch == 1
    depth = ln_mix_g.shape[0]
    alpha = (2 * depth) ** 0.25
    dh = attn_lambda.shape[-1]
    da_heads = d // (2 * dh)
    ml_heads = mlstm_gate_b.shape[-1] // 2
    qk_width = mlstm_conv_w.shape[-1] // 2
    dk = qk_width // ml_heads
    dv = d // ml_heads

    xf = x.reshape(s, d)
    xb = xf.astype(BF16)
    for i in range(depth):
        j = i // 2
        if i % 2 == 0:
            lambda_init = 0.8 - 0.6 * math.exp(-0.3 * i)
            lv = attn_lambda[j].astype(F32)
            lam = jnp.exp(jnp.sum(lv[0] * lv[1])) - jnp.exp(jnp.sum(lv[2] * lv[3])) + lambda_init
            proj = matmul(xb, attn_w_in, j, 0, 3 * d, BF16)
            o = diff_attention_core(proj, lam, attn_norm_g[j], lambda_init, da_heads, dh)
            mix = matmul(o, attn_w_out, j, 0, d, BF16)
        else:
            qk_pre = matmul(xb, mlstm_w_in, j, 0, 2 * qk_width, F32)
            vo = matmul(xb, mlstm_w_in, j, 2 * qk_width, 2 * d, BF16)
            gates = mlstm_gates(xb, mlstm_w_in[j][:, 2 * qk_width + 2 * d:], mlstm_gate_b[j], ml_heads)
            qk = causal_conv_silu(qk_pre, mlstm_conv_w[j], qk_width, dk ** -0.5)
            hcore = mlstm_core(qk, vo, gates.T, mlstm_norm_g[j], ml_heads, dk, dv)
            mix = matmul(hcore, mlstm_w_out, j, 0, d, BF16)
        xf, xb = residual_layer_norm(xf, mix, ln_mix_g[i], ln_mix_b[i], alpha)
        xf, xb = moe_block(xf, w_router, b_router, moe_w_gate[i].astype(BF16), moe_w_up[i].astype(BF16),
                           moe_w_down[i].astype(BF16), ln_ffn_g[i], ln_ffn_b[i], alpha)
    return xf.reshape(batch, s, d)
```

```python
import functools
import math

import jax
import jax.numpy as jnp
from jax import lax
from jax.experimental import pallas as pl
from jax.experimental.pallas import tpu as pltpu

F32 = jnp.float32
BF16 = jnp.bfloat16

LN_EPS = 1e-5
RMS_EPS = 1e-6
N_GROUPS = 4
TOP_K = 2
LOG2E = math.log2(math.e)

V7X_VMEM_BYTES = 64 * 1024 * 1024
VMEM_LIMIT = V7X_VMEM_BYTES - 8 * 1024 * 1024
LANE = 128


def _cparams(sem):
    return pltpu.CompilerParams(dimension_semantics=sem, vmem_limit_bytes=VMEM_LIMIT)


def _tile(n, pref):
    t = min(n, pref)
    while n % t:
        t //= 2
    return t


def _mm_kernel(a_ref, w_ref, o_ref):
    a = a_ref[...].astype(BF16)
    w = w_ref[...].astype(BF16)
    o_ref[...] = jnp.dot(a, w, preferred_element_type=F32).astype(o_ref.dtype)


def matmul(a, w3, layer, col0, n, out_dtype, tm=1024, tn=512):
    m, k = a.shape
    tm = _tile(m, tm)
    tn = _tile(n, tn)
    assert col0 % tn == 0
    cb = col0 // tn
    return pl.pallas_call(
        _mm_kernel,
        grid=(m // tm, n // tn),
        in_specs=[pl.BlockSpec((tm, k), lambda i, j: (i, 0)),
                  pl.BlockSpec((None, k, tn), lambda i, j: (layer, 0, cb + j))],
        out_specs=pl.BlockSpec((tm, tn), lambda i, j: (i, j)),
        out_shape=jax.ShapeDtypeStruct((m, n), out_dtype),
        compiler_params=_cparams(("parallel", "parallel")),
        name="proj_matmul",
    )(a, w3)


def _layer_norm_rows(z, g, b):
    mu = jnp.mean(z, axis=-1, keepdims=True)
    zc = z - mu
    var = jnp.mean(zc * zc, axis=-1, keepdims=True)
    return zc * lax.rsqrt(var + LN_EPS) * g + b


def _ln_kernel(x_ref, y_ref, g_ref, b_ref, of_ref, ob_ref, *, alpha):
    z = alpha * x_ref[...] + y_ref[...].astype(F32)
    out = _layer_norm_rows(z, g_ref[...], b_ref[...])
    of_ref[...] = out
    ob_ref[...] = out.astype(BF16)


def residual_layer_norm(x, y, g, b, alpha, tr=256):
    s, d = x.shape
    tr = _tile(s, tr)
    row = pl.BlockSpec((tr, d), lambda i: (i, 0))
    vec = pl.BlockSpec((1, d), lambda i: (0, 0))
    return pl.pallas_call(
        functools.partial(_ln_kernel, alpha=alpha),
        grid=(s // tr,),
        in_specs=[row, row, vec, vec],
        out_specs=[row, row],
        out_shape=[jax.ShapeDtypeStruct((s, d), F32), jax.ShapeDtypeStruct((s, d), BF16)],
        compiler_params=_cparams(("parallel",)),
        name="residual_layer_norm",
    )(x, y, g.reshape(1, d), b.reshape(1, d))


def _transpose_kernel(x_ref, o_ref):
    o_ref[...] = x_ref[...].astype(F32).T.astype(o_ref.dtype)


def transpose_cols(x, col0, n, t=512):
    s = x.shape[0]
    t = _tile(s, t)
    tc = _tile(n, t)
    assert col0 % tc == 0
    cb = col0 // tc
    return pl.pallas_call(
        _transpose_kernel,
        grid=(s // t, n // tc),
        in_specs=[pl.BlockSpec((t, tc), lambda i, j: (i, cb + j))],
        out_specs=pl.BlockSpec((tc, t), lambda i, j: (j, i)),
        out_shape=jax.ShapeDtypeStruct((n, s), x.dtype),
        compiler_params=_cparams(("parallel", "parallel")),
        name="transpose_cols",
    )(x)


def _attn_kernel(lam_ref, q_ref, k_ref, vt_ref, g_ref, o_ref, m_sc, l_sc, acc_sc, *, t, dh, q_scale, out_scale):
    i = pl.program_id(1)
    nt_dims = (((1,), (1,)), ((), ()))
    qb = (q_ref[...].astype(F32) * q_scale).astype(BF16)
    m_sc[...] = jnp.full(m_sc.shape, -jnp.inf, F32)
    l_sc[...] = jnp.zeros(l_sc.shape, F32)
    acc_sc[...] = jnp.zeros(acc_sc.shape, F32)

    def block(j, masked):
        off = pl.multiple_of(j * t, t)
        kb = k_ref[pl.ds(off, t), :]
        vtb = vt_ref[:, pl.ds(off, t)]
        for c in range(2):
            s = lax.dot_general(kb[:, c * dh:(c + 1) * dh], qb[:, c * dh:(c + 1) * dh], nt_dims,
                                preferred_element_type=F32)
            if masked:
                krow = lax.broadcasted_iota(jnp.int32, (t, t), 0)
                qcol = lax.broadcasted_iota(jnp.int32, (t, t), 1)
                s = jnp.where(krow <= qcol, s, -jnp.inf)
            m_prev = m_sc[c]
            m_new = jnp.maximum(m_prev, jnp.max(s, axis=0, keepdims=True))
            a = jnp.exp2(m_prev - m_new)
            p = jnp.exp2(s - m_new)
            l_sc[c] = a * l_sc[c] + jnp.sum(p, axis=0, keepdims=True)
            acc_sc[c] = a * acc_sc[c] + jnp.dot(vtb, p.astype(BF16), preferred_element_type=F32)
            m_sc[c] = m_new

    def body(j, carry):
        block(j, False)
        return carry

    lax.fori_loop(0, i, body, 0)
    block(i, True)

    lam = lam_ref[0, 0]
    o = acc_sc[0] / l_sc[0] - lam * (acc_sc[1] / l_sc[1])
    o = o * lax.rsqrt(jnp.mean(o * o, axis=0, keepdims=True) + RMS_EPS)
    o = o * (g_ref[...] * out_scale)
    o_ref[...] = o.T.astype(o_ref.dtype)


def diff_attention_core(proj, lam, norm_g, lambda_init, n_heads, dh, t=1024):
    s, d3 = proj.shape
    hd = 2 * dh
    d = n_heads * hd
    t = _tile(s, t)
    vt = transpose_cols(proj, 2 * d, d)
    kern = functools.partial(_attn_kernel, t=t, dh=dh, q_scale=(dh ** -0.5) * LOG2E,
                             out_scale=1.0 - lambda_init)
    return pl.pallas_call(
        kern,
        grid=(n_heads, s // t),
        in_specs=[pl.BlockSpec(memory_space=pltpu.SMEM),
                  pl.BlockSpec((t, hd), lambda h, i: (i, h)),
                  pl.BlockSpec((s, hd), lambda h, i: (0, n_heads + h)),
                  pl.BlockSpec((hd, s), lambda h, i: (h, 0)),
                  pl.BlockSpec((hd, 1), lambda h, i: (0, 0))],
        out_specs=pl.BlockSpec((t, hd), lambda h, i: (i, h)),
        out_shape=jax.ShapeDtypeStruct((s, d), BF16),
        scratch_shapes=[pltpu.VMEM((2, 1, t), F32), pltpu.VMEM((2, 1, t), F32), pltpu.VMEM((2, hd, t), F32)],
        compiler_params=_cparams(("parallel", "arbitrary")),
        name="diff_attention",
    )(lam.reshape(1, 1), proj, proj, vt, norm_g.reshape(hd, 1))


def _gates_kernel(a_ref, w_ref, b_ref, o_ref, *, n_heads):
    z = jnp.dot(a_ref[...], w_ref[...].astype(BF16), preferred_element_type=F32) + b_ref[...]
    log_sig = jnp.minimum(z, 0.0) - jnp.log(1.0 + jnp.exp(-jnp.abs(z)))
    col = lax.broadcasted_iota(jnp.int32, z.shape, 1)
    o_ref[...] = jnp.where(col >= n_heads, log_sig, z)


def mlstm_gates(a, w_g, bias, n_heads, tm=1024):
    m, k = a.shape
    n = w_g.shape[1]
    tm = _tile(m, tm)
    return pl.pallas_call(
        functools.partial(_gates_kernel, n_heads=n_heads),
        grid=(m // tm,),
        in_specs=[pl.BlockSpec((tm, k), lambda i: (i, 0)),
                  pl.BlockSpec((k, n), lambda i: (0, 0)),
                  pl.BlockSpec((1, n), lambda i: (0, 0))],
        out_specs=pl.BlockSpec((tm, n), lambda i: (i, 0)),
        out_shape=jax.ShapeDtypeStruct((m, n), F32),
        compiler_params=_cparams(("parallel",)),
        name="mlstm_gates",
    )(a, w_g, bias.reshape(1, n))


def _conv_kernel(u_ref, w_ref, o_ref, tail_sc, *, n_taps, q_tiles, q_scale):
    j = pl.program_id(0)
    ti = pl.program_id(1)

    @pl.when(ti == 0)
    def _():
        tail_sc[...] = jnp.zeros(tail_sc.shape, F32)

    u = u_ref[...]
    tr = u.shape[0]
    tail = tail_sc[...]
    row8 = lax.broadcasted_iota(jnp.int32, tail.shape, 0)
    acc = u * w_ref[n_taps - 1:n_taps, :]
    for sh in range(1, n_taps):
        rolled = pltpu.roll(u, sh, 0)
        head = jnp.where(row8 < sh, pltpu.roll(tail, sh, 0), rolled[:8])
        shifted = jnp.concatenate([head, rolled[8:]], axis=0)
        acc = acc + shifted * w_ref[n_taps - 1 - sh:n_taps - sh, :]
    tail_sc[...] = u[tr - 8:, :]
    y = acc * jax.nn.sigmoid(acc)
    y = y * jnp.where(j < q_tiles, q_scale, 1.0)
    o_ref[...] = y.astype(o_ref.dtype)


def causal_conv_silu(u, w, qk_width, q_scale, tr=512, tc=512):
    s, c = u.shape
    n_taps = w.shape[0]
    tr = _tile(s, tr)
    tc = _tile(qk_width, tc)
    kern = functools.partial(_conv_kernel, n_taps=n_taps, q_tiles=qk_width // tc, q_scale=q_scale)
    return pl.pallas_call(
        kern,
        grid=(c // tc, s // tr),
        in_specs=[pl.BlockSpec((tr, tc), lambda j, i: (i, j)),
                  pl.BlockSpec((n_taps, tc), lambda j, i: (0, j))],
        out_specs=pl.BlockSpec((tr, tc), lambda j, i: (i, j)),
        out_shape=jax.ShapeDtypeStruct((s, c), BF16),
        scratch_shapes=[pltpu.VMEM((8, tc), F32)],
        compiler_params=_cparams(("parallel", "arbitrary")),
        name="causal_conv_silu",
    )(u, w)


def _mlstm_kernel(q_ref, k_ref, v_ref, og_ref, gt_ref, ng_ref, o_ref, c_sc, m_sc, *, lc, dv, n_heads):
    h = pl.program_id(0)
    ci = pl.program_id(1)

    @pl.when(ci == 0)
    def _():
        c_sc[...] = jnp.zeros(c_sc.shape, F32)
        m_sc[...] = jnp.zeros(m_sc.shape, F32)

    ig_row = gt_ref[pl.ds(h, 1), :]
    lf_row = gt_ref[pl.ds(h + n_heads, 1), :]

    lane8 = lax.broadcasted_iota(jnp.int32, (8, lc), 1)
    b8 = jnp.broadcast_to(lf_row, (8, lc))
    d = 1
    while d < lc:
        b8 = b8 + jnp.where(lane8 >= d, pltpu.roll(b8, d, 1), 0.0)
        d *= 2
    b_row = b8[0:1, :]

    half = lc // 2
    rows = lax.broadcasted_iota(jnp.int32, (lc, lc), 0)
    cols = lax.broadcasted_iota(jnp.int32, (lc, lc), 1)
    stacked = jnp.where(rows < half, jnp.broadcast_to(b_row, (lc, lc)), jnp.broadcast_to(ig_row, (lc, lc)))
    stacked_t = stacked.T
    b_col = stacked_t[:, 0:1]
    ig_col = stacked_t[:, half:half + 1]

    m_prev = m_sc[...]
    dlog = jnp.where(cols <= rows, b_col - b_row + ig_row, -jnp.inf)
    inter = b_col + m_prev
    m_out = jnp.maximum(inter, jnp.max(dlog, axis=-1, keepdims=True))
    dw = jnp.exp(dlog - m_out)
    inter_w = jnp.exp(inter - m_out)

    q = q_ref[...]
    k = k_ref[...]
    ones_col = (lax.broadcasted_iota(jnp.int32, (lc, LANE), 1) == 0).astype(BF16)
    v_ext = jnp.concatenate([v_ref[...], ones_col], axis=1)

    scores = lax.dot_general(q, k, (((1,), (1,)), ((), ())), preferred_element_type=F32) * dw
    c_ext = c_sc[...]
    num_ext = inter_w * jnp.dot(q, c_ext.astype(BF16), preferred_element_type=F32) \
        + jnp.dot(scores.astype(BF16), v_ext, preferred_element_type=F32)
    num = num_ext[:, :dv]
    den = num_ext[:, dv:dv + 1]
    hid = num / jnp.maximum(jnp.abs(den), jnp.exp(-m_out))

    b_last = b_row[:, lc - 1:lc]
    g_col = b_last - b_col + ig_col
    m_new = jnp.maximum(b_last + m_prev, jnp.max(g_col, axis=0, keepdims=True))
    decay = jnp.exp(b_last + m_prev - m_new)
    wg_col = jnp.exp(g_col - m_new)
    vw_ext = (v_ext.astype(F32) * wg_col).astype(BF16)
    k_t = k.astype(F32).T.astype(BF16)
    c_sc[...] = decay * c_ext + jnp.dot(k_t, vw_ext, preferred_element_type=F32)
    m_sc[...] = m_new

    hn = hid * lax.rsqrt(jnp.mean(hid * hid, axis=-1, keepdims=True) + RMS_EPS)
    o_ref[...] = (hn * ng_ref[...] * jax.nn.sigmoid(og_ref[...].astype(F32))).astype(o_ref.dtype)


def mlstm_core(qk, vo, gates_t, norm_g, n_heads, dk, dv, lc=256):
    s = qk.shape[0]
    d = n_heads * dv
    lc = _tile(s, lc)
    kern = functools.partial(_mlstm_kernel, lc=lc, dv=dv, n_heads=n_heads)
    return pl.pallas_call(
        kern,
        grid=(n_heads, s // lc),
        in_specs=[pl.BlockSpec((lc, dk), lambda h, c: (c, h)),
                  pl.BlockSpec((lc, dk), lambda h, c: (c, n_heads + h)),
                  pl.BlockSpec((lc, dv), lambda h, c: (c, h)),
                  pl.BlockSpec((lc, dv), lambda h, c: (c, n_heads + h)),
                  pl.BlockSpec((2 * n_heads, lc), lambda h, c: (0, c)),
                  pl.BlockSpec((1, dv), lambda h, c: (0, h))],
        out_specs=pl.BlockSpec((lc, dv), lambda h, c: (c, h)),
        out_shape=jax.ShapeDtypeStruct((s, d), BF16),
        scratch_shapes=[pltpu.VMEM((dk, dv + LANE), F32), pltpu.VMEM((1, 1), F32)],
        compiler_params=_cparams(("parallel", "arbitrary")),
        name="mlstm_chunks",
    )(qk, qk, vo, vo, gates_t, norm_g.reshape(1, d))


def _split_bf16(x):
    hi = x.astype(BF16)
    lo = (x - hi.astype(F32)).astype(BF16)
    return hi, lo


def _router_kernel(x_ref, wt_ref, br_ref, ri_ref, rw_ref, cnt_ref, lg_sc, base_sc, *, n_exp, tm):
    step = pl.program_id(0)
    per_group = n_exp // N_GROUPS

    @pl.when(step == 0)
    def _():
        base_sc[...] = jnp.zeros(base_sc.shape, F32)

    nt = (((1,), (1,)), ((), ()))
    x_hi, x_lo = _split_bf16(x_ref[...])
    w_hi, w_lo = _split_bf16(wt_ref[...])
    lg_sc[...] = (lax.dot_general(w_hi, x_hi, nt, preferred_element_type=F32)
                  + lax.dot_general(w_hi, x_lo, nt, preferred_element_type=F32)
                  + lax.dot_general(w_lo, x_hi, nt, preferred_element_type=F32))

    logit = [lg_sc[e:e + 1, :] for e in range(n_exp)]
    mx = functools.reduce(jnp.maximum, logit)
    ex = [jnp.exp(l - mx) for l in logit]
    inv = 1.0 / functools.reduce(lambda a, b: a + b, ex)
    prob = [e * inv for e in ex]
    sel = [prob[e] + br_ref[e:e + 1, :] for e in range(n_exp)]

    def top2_sum(a):
        hi01, lo01 = jnp.maximum(a[0], a[1]), jnp.minimum(a[0], a[1])
        hi23, lo23 = jnp.maximum(a[2], a[3]), jnp.minimum(a[2], a[3])
        return jnp.maximum(hi01, hi23) + jnp.maximum(jnp.minimum(hi01, hi23), jnp.maximum(lo01, lo23))

    assert per_group == 4
    gscore = [top2_sum(sel[g * per_group:(g + 1) * per_group]) for g in range(N_GROUPS)]
    best, grp = gscore[0], jnp.zeros_like(gscore[0], dtype=jnp.int32)
    for g in range(1, N_GROUPS):
        upd = gscore[g] > best
        grp = jnp.where(upd, g, grp)
        best = jnp.where(upd, gscore[g], best)

    def pick(vals, i):
        out = vals[i]
        for g in range(1, N_GROUPS):
            out = jnp.where(grp == g, vals[g * per_group + i], out)
        return out

    gs = [pick(sel, i) for i in range(per_group)]
    gp = [pick(prob, i) for i in range(per_group)]
    v1, i1, p1 = gs[0], jnp.zeros_like(grp), gp[0]
    for i in range(1, per_group):
        upd = gs[i] > v1
        v1, i1, p1 = jnp.where(upd, gs[i], v1), jnp.where(upd, i, i1), jnp.where(upd, gp[i], p1)
    v2 = jnp.full_like(v1, -jnp.inf)
    i2, p2 = jnp.zeros_like(grp), jnp.zeros_like(p1)
    for i in range(per_group):
        upd = (i1 != i) & (gs[i] > v2)
        v2, i2, p2 = jnp.where(upd, gs[i], v2), jnp.where(upd, i, i2), jnp.where(upd, gp[i], p2)
    e1 = grp * per_group + i1
    e2 = grp * per_group + i2
    wsum = p1 + p2
    rw_ref[...] = jnp.zeros(rw_ref.shape, F32)
    rw_ref[0:1, :] = p1 / wsum
    rw_ref[1:2, :] = p2 / wsum

    for e in range(n_exp):
        lg_sc[e:e + 1, :] = ((e1 == e) | (e2 == e)).astype(F32)
    mask = lg_sc[...]
    r_i = lax.broadcasted_iota(jnp.int32, (tm, tm), 0)
    c_i = lax.broadcasted_iota(jnp.int32, (tm, tm), 1)
    upper = (r_i < c_i).astype(BF16)
    lg_sc[...] = jnp.dot(mask.astype(BF16), upper, preferred_element_type=F32) + base_sc[...]
    new_base = base_sc[...] + jnp.sum(mask, axis=1, keepdims=True)
    base_sc[...] = new_base
    cnt_ref[...] = jnp.broadcast_to(new_base, cnt_ref.shape).astype(jnp.int32)
    r1 = jnp.zeros_like(p1)
    r2 = jnp.zeros_like(p1)
    for e in range(n_exp):
        excl_e = lg_sc[e:e + 1, :]
        r1 = jnp.where(e1 == e, excl_e, r1)
        r2 = jnp.where(e2 == e, excl_e, r2)
    ri_ref[...] = jnp.zeros(ri_ref.shape, jnp.int32)
    ri_ref[0:1, :] = e1
    ri_ref[1:2, :] = e2
    ri_ref[2:3, :] = r1.astype(jnp.int32)
    ri_ref[3:4, :] = r2.astype(jnp.int32)


def router(x, w_router, b_router, tm=512):
    s, d = x.shape
    n_exp = w_router.shape[1]
    tm = _tile(s, tm)
    kern = functools.partial(_router_kernel, n_exp=n_exp, tm=tm)
    return pl.pallas_call(
        kern,
        grid=(s // tm,),
        in_specs=[pl.BlockSpec((tm, d), lambda i: (i, 0)),
                  pl.BlockSpec((n_exp, d), lambda i: (0, 0)),
                  pl.BlockSpec((n_exp, 1), lambda i: (0, 0))],
        out_specs=[pl.BlockSpec((8, tm), lambda i: (0, i)),
                   pl.BlockSpec((8, tm), lambda i: (0, i)),
                   pl.BlockSpec((n_exp, LANE), lambda i: (0, 0))],
        out_shape=[jax.ShapeDtypeStruct((8, s), jnp.int32), jax.ShapeDtypeStruct((8, s), F32),
                   jax.ShapeDtypeStruct((n_exp, LANE), jnp.int32)],
        scratch_shapes=[pltpu.VMEM((n_exp, tm), F32), pltpu.VMEM((n_exp, 1), F32)],
        compiler_params=_cparams(("arbitrary",)),
        name="moe_router",
    )(x, w_router.T, b_router.reshape(n_exp, 1))


def _dispatch_kernel(dest_ref, pad_ref, x_ref, xs_ref, zero_sc, sem, *, tt, n_tok, n_exp):
    i = pl.program_id(0)
    base = i * tt

    def row_copy(src, r_src, r_dst):
        return pltpu.make_async_copy(src.at[pl.ds(r_src, 1), :], xs_ref.at[pl.ds(r_dst, 1), :], sem)

    @pl.when(i == 0)
    def _():
        zero_sc[...] = jnp.zeros(zero_sc.shape, zero_sc.dtype)
        for e in range(n_exp):
            lo, hi = pad_ref[e], pad_ref[n_exp + e]
            lax.fori_loop(lo, hi, lambda r, c: (row_copy(zero_sc, 0, r).start(), c)[1], 0)
        for e in range(n_exp):
            lo, hi = pad_ref[e], pad_ref[n_exp + e]
            lax.fori_loop(lo, hi, lambda r, c: (row_copy(zero_sc, 0, r).wait(), c)[1], 0)

    def start_tok(r, c):
        row_copy(x_ref, r, dest_ref[base + r]).start()
        row_copy(x_ref, r, dest_ref[n_tok + base + r]).start()
        return c

    def wait_tok(r, c):
        row_copy(x_ref, r, dest_ref[base + r]).wait()
        row_copy(x_ref, r, dest_ref[n_tok + base + r]).wait()
        return c

    lax.fori_loop(0, tt, start_tok, 0)
    lax.fori_loop(0, tt, wait_tok, 0)


def dispatch(x, dest, pad_bounds, n_slots, n_exp, tt=256):
    s, d = x.shape
    tt = _tile(s, tt)
    kern = functools.partial(_dispatch_kernel, tt=tt, n_tok=s, n_exp=n_exp)
    return pl.pallas_call(
        kern,
        grid_spec=pltpu.PrefetchScalarGridSpec(
            num_scalar_prefetch=2,
            grid=(s // tt,),
            in_specs=[pl.BlockSpec((tt, d), lambda i, dr, pr: (i, 0))],
            out_specs=pl.BlockSpec(memory_space=pl.ANY),
            scratch_shapes=[pltpu.VMEM((8, d), x.dtype), pltpu.SemaphoreType.DMA],
        ),
        out_shape=jax.ShapeDtypeStruct((n_slots, d), x.dtype),
        compiler_params=_cparams(("arbitrary",)),
        name="moe_dispatch",
    )(dest, pad_bounds, x)


def _cast_kernel(x_ref, o_ref):
    o_ref[...] = x_ref[...].astype(o_ref.dtype)


def cast_layer_weights(w4, layer, dtype=BF16):
    _, n_exp, a, b = w4.shape
    return pl.pallas_call(
        _cast_kernel,
        grid=(n_exp,),
        in_specs=[pl.BlockSpec((None, None, a, b), lambda e: (layer, e, 0, 0))],
        out_specs=pl.BlockSpec((None, a, b), lambda e: (e, 0, 0)),
        out_shape=jax.ShapeDtypeStruct((n_exp, a, b), dtype),
        compiler_params=_cparams(("parallel",)),
        name="cast_weights",
    )(w4)


def _expert_kernel(te_ref, ts_ref, nv_ref, xs_ref, wg_ref, wu_ref, wd_ref, ys_ref):
    @pl.when(pl.program_id(0) < nv_ref[0])
    def _():
        x = xs_ref[...].astype(BF16)
        g = jnp.dot(x, wg_ref[...], preferred_element_type=F32)
        u = jnp.dot(x, wu_ref[...], preferred_element_type=F32)
        hmid = (g * jax.nn.sigmoid(g) * u).astype(BF16)
        ys_ref[...] = jnp.dot(hmid, wd_ref[...], preferred_element_type=F32).astype(ys_ref.dtype)


def expert_mlp(xs, w_gate, w_up, w_down, tile_expert, tile_src, n_valid, tm):
    n_slots, d = xs.shape
    _, _, f = w_gate.shape
    n_tiles = n_slots // tm
    return pl.pallas_call(
        _expert_kernel,
        grid_spec=pltpu.PrefetchScalarGridSpec(
            num_scalar_prefetch=3,
            grid=(n_tiles,),
            in_specs=[pl.BlockSpec((tm, d), lambda t, te, ts, nv: (ts[t], 0)),
                      pl.BlockSpec((None, d, f), lambda t, te, ts, nv: (te[t], 0, 0)),
                      pl.BlockSpec((None, d, f), lambda t, te, ts, nv: (te[t], 0, 0)),
                      pl.BlockSpec((None, f, d), lambda t, te, ts, nv: (te[t], 0, 0))],
            out_specs=pl.BlockSpec((tm, d), lambda t, te, ts, nv: (ts[t], 0)),
        ),
        out_shape=jax.ShapeDtypeStruct((n_slots, d), F32),
        compiler_params=_cparams(("arbitrary",)),
        name="moe_experts",
    )(tile_expert, tile_src, n_valid, xs, w_gate, w_up, w_down)


def _combine_kernel(dest_ref, x_ref, w_ref, g_ref, b_ref, ys_ref, of_ref, ob_ref, buf, sem, *, tt, n_tok, alpha):
    i = pl.program_id(0)
    n_steps = pl.num_programs(0)

    def row_copy(step, slot, kk, r):
        tok = step * tt + r
        return pltpu.make_async_copy(ys_ref.at[pl.ds(dest_ref[kk * n_tok + tok], 1), :],
                                     buf.at[slot, kk, pl.ds(r, 1), :], sem.at[slot])

    def start_tile(step, slot):
        def body(r, c):
            row_copy(step, slot, 0, r).start()
            row_copy(step, slot, 1, r).start()
            return c
        lax.fori_loop(0, tt, body, 0)

    def wait_tile(step, slot):
        def body(r, c):
            row_copy(step, slot, 0, r).wait()
            row_copy(step, slot, 1, r).wait()
            return c
        lax.fori_loop(0, tt, body, 0)

    slot = i % 2

    @pl.when(i == 0)
    def _():
        start_tile(0, 0)

    @pl.when(i + 1 < n_steps)
    def _():
        start_tile(i + 1, 1 - slot)

    wait_tile(i, slot)
    w = w_ref[...]
    y = w[:, 0:1] * buf[slot, 0] + w[:, 1:2] * buf[slot, 1]
    out = _layer_norm_rows(alpha * x_ref[...] + y, g_ref[...], b_ref[...])
    of_ref[...] = out
    ob_ref[...] = out.astype(BF16)


def combine_layer_norm(x, ys, dest, w_tok, g, b, alpha, tt=256):
    s, d = x.shape
    tt = _tile(s, tt)
    kern = functools.partial(_combine_kernel, tt=tt, n_tok=s, alpha=alpha)
    row = lambda i, dr: (i, 0)
    fixed = lambda i, dr: (0, 0)
    return pl.pallas_call(
        kern,
        grid_spec=pltpu.PrefetchScalarGridSpec(
            num_scalar_prefetch=1,
            grid=(s // tt,),
            in_specs=[pl.BlockSpec((tt, d), row),
                      pl.BlockSpec((tt, TOP_K), row),
                      pl.BlockSpec((1, d), fixed),
                      pl.BlockSpec((1, d), fixed),
                      pl.BlockSpec(memory_space=pl.ANY)],
            out_specs=[pl.BlockSpec((tt, d), row), pl.BlockSpec((tt, d), row)],
            scratch_shapes=[pltpu.VMEM((2, TOP_K, tt, d), F32), pltpu.SemaphoreType.DMA((2,))],
        ),
        out_shape=[jax.ShapeDtypeStruct((s, d), F32), jax.ShapeDtypeStruct((s, d), BF16)],
        compiler_params=_cparams(("arbitrary",)),
        name="moe_combine_layer_norm",
    )(dest, x, w_tok, g.reshape(1, d), b.reshape(1, d), ys)


def moe_block(x_f32, w_router, b_router, w_gate, w_up, w_down, ln_g, ln_b, alpha, tm=256):
    s, d = x_f32.shape
    n_exp = w_router.shape[1]
    tm = _tile(s, tm)
    route_i, route_w, counts = router(x_f32, w_router, b_router)
    counts = counts[:, 0]
    padded = ((counts + tm - 1) // tm) * tm
    ends = jnp.cumsum(padded)
    offsets = ends - padded
    e1, e2, r1, r2 = route_i[0], route_i[1], route_i[2], route_i[3]
    dest = jnp.concatenate([offsets[e1] + r1, offsets[e2] + r2]).astype(jnp.int32)
    pad_bounds = jnp.concatenate([offsets + counts, ends]).astype(jnp.int32)
    n_slots = TOP_K * s + n_exp * tm
    n_tiles = n_slots // tm
    n_valid = (ends[-1] // tm).astype(jnp.int32)
    tile_start = jnp.arange(n_tiles, dtype=jnp.int32) * tm
    tile_expert = jnp.minimum(jnp.sum(tile_start[:, None] >= ends[None, :], axis=1), n_exp - 1).astype(jnp.int32)
    tile_src = jnp.minimum(jnp.arange(n_tiles, dtype=jnp.int32), n_valid - 1)
    tile_expert = tile_expert[tile_src]

    xs = dispatch(x_f32, dest, pad_bounds, n_slots, n_exp)
    ys = expert_mlp(xs, w_gate, w_up, w_down, tile_expert, tile_src, n_valid.reshape(1), tm)
    w_tok = jnp.stack([route_w[0], route_w[1]], axis=1)
    return combine_layer_norm(x_f32, ys, dest, w_tok, ln_g, ln_b, alpha)


def kernel(x, attn_w_in, attn_lambda, attn_norm_g, attn_w_out, mlstm_w_in, mlstm_gate_b, mlstm_conv_w,
           mlstm_norm_g, mlstm_w_out, ln_mix_g, ln_mix_b, ln_ffn_g, ln_ffn_b, w_router, b_router,
           moe_w_gate, moe_w_up, moe_w_down):
    batch, s, d = x.shape
    assert batch == 1
    depth = ln_mix_g.shape[0]
    alpha = (2 * depth) ** 0.25
    dh = attn_lambda.shape[-1]
    da_heads = d // (2 * dh)
    ml_heads = mlstm_gate_b.shape[-1] // 2
    qk_width = mlstm_conv_w.shape[-1] // 2
    dk = qk_width // ml_heads
    dv = d // ml_heads

    xf = x.reshape(s, d)
    xb = xf.astype(BF16)
    for i in range(depth):
        j = i // 2
        if i % 2 == 0:
            lambda_init = 0.8 - 0.6 * math.exp(-0.3 * i)
            lv = attn_lambda[j].astype(F32)
            lam = jnp.exp(jnp.sum(lv[0] * lv[1])) - jnp.exp(jnp.sum(lv[2] * lv[3])) + lambda_init
            proj = matmul(xb, attn_w_in, j, 0, 3 * d, BF16)
            o = diff_attention_core(proj, lam, attn_norm_g[j], lambda_init, da_heads, dh)
            mix = matmul(o, attn_w_out, j, 0, d, BF16)
        else:
            qk_pre = matmul(xb, mlstm_w_in, j, 0, 2 * qk_width, F32)
            vo = matmul(xb, mlstm_w_in, j, 2 * qk_width, 2 * d, BF16)
            gates = mlstm_gates(xb, mlstm_w_in[j][:, 2 * qk_width + 2 * d:], mlstm_gate_b[j], ml_heads)
            qk = causal_conv_silu(qk_pre, mlstm_conv_w[j], qk_width, dk ** -0.5)
            hcore = mlstm_core(qk, vo, gates.T, mlstm_norm_g[j], ml_heads, dk, dv)
            mix = matmul(hcore, mlstm_w_out, j, 0, d, BF16)
        xf, xb = residual_layer_norm(xf, mix, ln_mix_g[i], ln_mix_b[i], alpha)
        xf, xb = moe_block(xf, w_router, b_router, cast_layer_weights(moe_w_gate, i), cast_layer_weights(moe_w_up, i),
                           cast_layer_weights(moe_w_down, i), ln_ffn_g[i], ln_ffn_b[i], alpha)
    return xf.reshape(batch, s, d)
```

```python
import functools
import math

import jax
import jax.numpy as jnp
from jax import lax
from jax.experimental import pallas as pl
from jax.experimental.pallas import tpu as pltpu

F32 = jnp.float32
BF16 = jnp.bfloat16

LN_EPS = 1e-5
RMS_EPS = 1e-6
N_GROUPS = 4
TOP_K = 2
LOG2E = math.log2(math.e)

V7X_VMEM_BYTES = 64 * 1024 * 1024
VMEM_LIMIT = V7X_VMEM_BYTES - 8 * 1024 * 1024
LANE = 128


def _cparams(sem):
    return pltpu.CompilerParams(dimension_semantics=sem, vmem_limit_bytes=VMEM_LIMIT)


def _tile(n, pref):
    t = min(n, pref)
    while n % t:
        t //= 2
    return t


NT_DIMS = (((1,), (1,)), ((), ()))


def _mm_kernel(a_ref, w_ref, o_ref, *, w_transposed):
    a = a_ref[...].astype(BF16)
    w = w_ref[...].astype(BF16)
    if w_transposed:
        acc = lax.dot_general(a, w, NT_DIMS, preferred_element_type=F32)
    else:
        acc = jnp.dot(a, w, preferred_element_type=F32)
    o_ref[...] = acc.astype(o_ref.dtype)


def matmul(a, w3, layer, col0, n, out_dtype, tm=1024, tn=512, w_transposed=False):
    m, k = a.shape
    tm = _tile(m, tm)
    tn = _tile(n, tn)
    assert col0 % tn == 0
    cb = col0 // tn
    if w_transposed:
        w_spec = pl.BlockSpec((None, tn, k), lambda i, j: (layer, cb + j, 0))
    else:
        w_spec = pl.BlockSpec((None, k, tn), lambda i, j: (layer, 0, cb + j))
    return pl.pallas_call(
        functools.partial(_mm_kernel, w_transposed=w_transposed),
        grid=(m // tm, n // tn),
        in_specs=[pl.BlockSpec((tm, k), lambda i, j: (i, 0)), w_spec],
        out_specs=pl.BlockSpec((tm, tn), lambda i, j: (i, j)),
        out_shape=jax.ShapeDtypeStruct((m, n), out_dtype),
        compiler_params=_cparams(("parallel", "parallel")),
        name="proj_matmul",
    )(a, w3)


def _layer_norm_rows(z, g, b):
    mu = jnp.mean(z, axis=-1, keepdims=True)
    zc = z - mu
    var = jnp.mean(zc * zc, axis=-1, keepdims=True)
    return zc * lax.rsqrt(var + LN_EPS) * g + b


def _ln_kernel(x_ref, y_ref, g_ref, b_ref, of_ref, ob_ref, *, alpha):
    z = alpha * x_ref[...] + y_ref[...].astype(F32)
    out = _layer_norm_rows(z, g_ref[...], b_ref[...])
    of_ref[...] = out
    ob_ref[...] = out.astype(BF16)


def residual_layer_norm(x, y, g, b, alpha, tr=256):
    s, d = x.shape
    tr = _tile(s, tr)
    row = pl.BlockSpec((tr, d), lambda i: (i, 0))
    vec = pl.BlockSpec((1, d), lambda i: (0, 0))
    return pl.pallas_call(
        functools.partial(_ln_kernel, alpha=alpha),
        grid=(s // tr,),
        in_specs=[row, row, vec, vec],
        out_specs=[row, row],
        out_shape=[jax.ShapeDtypeStruct((s, d), F32), jax.ShapeDtypeStruct((s, d), BF16)],
        compiler_params=_cparams(("parallel",)),
        name="residual_layer_norm",
    )(x, y, g.reshape(1, d), b.reshape(1, d))


def _transpose_kernel(x_ref, o_ref):
    o_ref[...] = x_ref[...].astype(F32).T.astype(o_ref.dtype)


def transpose_cols(x, col0, n, t=512):
    s = x.shape[0]
    t = _tile(s, t)
    tc = _tile(n, t)
    assert col0 % tc == 0
    cb = col0 // tc
    return pl.pallas_call(
        _transpose_kernel,
        grid=(s // t, n // tc),
        in_specs=[pl.BlockSpec((t, tc), lambda i, j: (i, cb + j))],
        out_specs=pl.BlockSpec((tc, t), lambda i, j: (j, i)),
        out_shape=jax.ShapeDtypeStruct((n, s), x.dtype),
        compiler_params=_cparams(("parallel", "parallel")),
        name="transpose_cols",
    )(x)


def _attn_kernel(lam_ref, q_ref, k_ref, vt_ref, g_ref, o_ref, m_sc, l_sc, acc_sc, sa_sc, sb_sc, *,
                 t, dh, q_scale, out_scale):
    n = pl.program_id(1)
    nt_dims = (((1,), (1,)), ((), ()))
    qb = (q_ref[...].astype(F32) * q_scale).astype(BF16)
    m_sc[...] = jnp.full(m_sc.shape, -jnp.inf, F32)
    l_sc[...] = jnp.zeros(l_sc.shape, F32)
    acc_sc[...] = jnp.zeros(acc_sc.shape, F32)

    def scores(j, dst):
        off = pl.multiple_of(j * t, t)
        kb = k_ref[pl.ds(off, t), :]
        for c in range(2):
            dst[c] = lax.dot_general(kb[:, c * dh:(c + 1) * dh], qb[:, c * dh:(c + 1) * dh], nt_dims,
                                     preferred_element_type=F32)

    def accumulate(j, src, masked):
        off = pl.multiple_of(j * t, t)
        vtb = vt_ref[:, pl.ds(off, t)]
        for c in range(2):
            s = src[c]
            if masked:
                krow = lax.broadcasted_iota(jnp.int32, (t, t), 0)
                qcol = lax.broadcasted_iota(jnp.int32, (t, t), 1)
                s = jnp.where(krow <= qcol, s, -jnp.inf)
            m_prev = m_sc[c]
            m_new = jnp.maximum(m_prev, jnp.max(s, axis=0, keepdims=True))
            a = jnp.exp2(m_prev - m_new)
            p = jnp.exp2(s - m_new)
            l_sc[c] = a * l_sc[c] + jnp.sum(p, axis=0, keepdims=True)
            acc_sc[c] = a * acc_sc[c] + jnp.dot(vtb, p.astype(BF16), preferred_element_type=F32)
            m_sc[c] = m_new

    scores(0, sa_sc)

    def pair(pp, carry):
        j = 2 * pp
        scores(j + 1, sb_sc)
        accumulate(j, sa_sc, False)
        scores(j + 2, sa_sc)
        accumulate(j + 1, sb_sc, False)
        return carry

    lax.fori_loop(0, n // 2, pair, 0)

    @pl.when(n % 2 == 0)
    def _():
        accumulate(n, sa_sc, True)

    @pl.when(n % 2 == 1)
    def _():
        scores(n, sb_sc)
        accumulate(n - 1, sa_sc, False)
        accumulate(n, sb_sc, True)

    lam = lam_ref[0, 0]
    o = acc_sc[0] / l_sc[0] - lam * (acc_sc[1] / l_sc[1])
    o = o * lax.rsqrt(jnp.mean(o * o, axis=0, keepdims=True) + RMS_EPS)
    o = o * (g_ref[...] * out_scale)
    o_ref[...] = o.T.astype(o_ref.dtype)


def diff_attention_core(proj, lam, norm_g, lambda_init, n_heads, dh, t=512):
    s, d3 = proj.shape
    hd = 2 * dh
    d = n_heads * hd
    t = _tile(s, t)
    vt = transpose_cols(proj, 2 * d, d)
    kern = functools.partial(_attn_kernel, t=t, dh=dh, q_scale=(dh ** -0.5) * LOG2E,
                             out_scale=1.0 - lambda_init)
    return pl.pallas_call(
        kern,
        grid=(n_heads, s // t),
        in_specs=[pl.BlockSpec(memory_space=pltpu.SMEM),
                  pl.BlockSpec((t, hd), lambda h, i: (i, h)),
                  pl.BlockSpec((s, hd), lambda h, i: (0, n_heads + h)),
                  pl.BlockSpec((hd, s), lambda h, i: (h, 0)),
                  pl.BlockSpec((hd, 1), lambda h, i: (0, 0))],
        out_specs=pl.BlockSpec((t, hd), lambda h, i: (i, h)),
        out_shape=jax.ShapeDtypeStruct((s, d), BF16),
        scratch_shapes=[pltpu.VMEM((2, 1, t), F32), pltpu.VMEM((2, 1, t), F32), pltpu.VMEM((2, hd, t), F32),
                        pltpu.VMEM((2, t, t), F32), pltpu.VMEM((2, t, t), F32)],
        compiler_params=_cparams(("parallel", "arbitrary")),
        name="diff_attention",
    )(lam.reshape(1, 1), proj, proj, vt, norm_g.reshape(hd, 1))


def _gates_kernel(a_ref, w_ref, b_ref, o_ref, *, n_heads):
    z = lax.dot_general(a_ref[...], w_ref[...].astype(BF16), NT_DIMS, preferred_element_type=F32) + b_ref[...]
    log_sig = jnp.minimum(z, 0.0) - jnp.log(1.0 + jnp.exp(-jnp.abs(z)))
    col = lax.broadcasted_iota(jnp.int32, z.shape, 1)
    o_ref[...] = jnp.where(col >= n_heads, log_sig, z)


def mlstm_gates(a, wt3, layer, row0, bias, n_heads, tm=1024):
    m, k = a.shape
    n = 2 * n_heads
    assert row0 % n == 0
    rb = row0 // n
    tm = _tile(m, tm)
    return pl.pallas_call(
        functools.partial(_gates_kernel, n_heads=n_heads),
        grid=(m // tm,),
        in_specs=[pl.BlockSpec((tm, k), lambda i: (i, 0)),
                  pl.BlockSpec((None, n, k), lambda i: (layer, rb, 0)),
                  pl.BlockSpec((1, n), lambda i: (0, 0))],
        out_specs=pl.BlockSpec((tm, n), lambda i: (i, 0)),
        out_shape=jax.ShapeDtypeStruct((m, n), F32),
        compiler_params=_cparams(("parallel",)),
        name="mlstm_gates",
    )(a, wt3, bias.reshape(1, n))


def _conv_kernel(u_ref, w_ref, o_ref, tail_sc, *, n_taps, q_tiles, q_scale):
    j = pl.program_id(0)
    ti = pl.program_id(1)

    @pl.when(ti == 0)
    def _():
        tail_sc[...] = jnp.zeros(tail_sc.shape, F32)

    u = u_ref[...]
    tr = u.shape[0]
    tail = tail_sc[...]
    row8 = lax.broadcasted_iota(jnp.int32, tail.shape, 0)
    acc = u * w_ref[n_taps - 1:n_taps, :]
    for sh in range(1, n_taps):
        rolled = pltpu.roll(u, sh, 0)
        head = jnp.where(row8 < sh, pltpu.roll(tail, sh, 0), rolled[:8])
        shifted = jnp.concatenate([head, rolled[8:]], axis=0)
        acc = acc + shifted * w_ref[n_taps - 1 - sh:n_taps - sh, :]
    tail_sc[...] = u[tr - 8:, :]
    y = acc * jax.nn.sigmoid(acc)
    y = y * jnp.where(j < q_tiles, q_scale, 1.0)
    o_ref[...] = y.astype(o_ref.dtype)


def causal_conv_silu(u, w, qk_width, q_scale, tr=512, tc=512):
    s, c = u.shape
    n_taps = w.shape[0]
    tr = _tile(s, tr)
    tc = _tile(qk_width, tc)
    kern = functools.partial(_conv_kernel, n_taps=n_taps, q_tiles=qk_width // tc, q_scale=q_scale)
    return pl.pallas_call(
        kern,
        grid=(c // tc, s // tr),
        in_specs=[pl.BlockSpec((tr, tc), lambda j, i: (i, j)),
                  pl.BlockSpec((n_taps, tc), lambda j, i: (0, j))],
        out_specs=pl.BlockSpec((tr, tc), lambda j, i: (i, j)),
        out_shape=jax.ShapeDtypeStruct((s, c), BF16),
        scratch_shapes=[pltpu.VMEM((8, tc), F32)],
        compiler_params=_cparams(("parallel", "arbitrary")),
        name="causal_conv_silu",
    )(u, w)


def _mlstm_kernel(q_ref, k_ref, v_ref, og_ref, gt_ref, ng_ref, o_ref, c_sc, m_sc, *, lc, dk, dv, n_heads, hps):
    hg = pl.program_id(0)
    ci = pl.program_id(1)

    @pl.when(ci == 0)
    def _():
        c_sc[...] = jnp.zeros(c_sc.shape, F32)
        m_sc[...] = jnp.zeros(m_sc.shape, F32)

    lane8 = lax.broadcasted_iota(jnp.int32, (8, lc), 1)
    half = lc // 2
    rows = lax.broadcasted_iota(jnp.int32, (lc, lc), 0)
    cols = lax.broadcasted_iota(jnp.int32, (lc, lc), 1)
    ones_col = (lax.broadcasted_iota(jnp.int32, (lc, LANE), 1) == 0).astype(BF16)

    for hh in range(hps):
        h = hg * hps + hh
        ig_row = gt_ref[pl.ds(h, 1), :]
        lf_row = gt_ref[pl.ds(h + n_heads, 1), :]

        b8 = jnp.broadcast_to(lf_row, (8, lc))
        d = 1
        while d < lc:
            b8 = b8 + jnp.where(lane8 >= d, pltpu.roll(b8, d, 1), 0.0)
            d *= 2
        b_row = b8[0:1, :]

        stacked = jnp.where(rows < half, jnp.broadcast_to(b_row, (lc, lc)), jnp.broadcast_to(ig_row, (lc, lc)))
        stacked_t = stacked.T
        b_col = stacked_t[:, 0:1]
        ig_col = stacked_t[:, half:half + 1]

        m_prev = m_sc[hh]
        dlog = jnp.where(cols <= rows, b_col - b_row + ig_row, -jnp.inf)
        inter = b_col + m_prev
        m_out = jnp.maximum(inter, jnp.max(dlog, axis=-1, keepdims=True))
        dw = jnp.exp(dlog - m_out)
        inter_w = jnp.exp(inter - m_out)

        q = q_ref[:, hh * dk:(hh + 1) * dk]
        k = k_ref[:, hh * dk:(hh + 1) * dk]
        v_ext = jnp.concatenate([v_ref[:, hh * dv:(hh + 1) * dv], ones_col], axis=1)

        scores = lax.dot_general(q, k, NT_DIMS, preferred_element_type=F32) * dw
        c_ext = c_sc[hh]
        num_ext = inter_w * jnp.dot(q, c_ext.astype(BF16), preferred_element_type=F32) \
            + jnp.dot(scores.astype(BF16), v_ext, preferred_element_type=F32)
        num = num_ext[:, :dv]
        den = num_ext[:, dv:dv + 1]
        hid = num / jnp.maximum(jnp.abs(den), jnp.exp(-m_out))

        b_last = b_row[:, lc - 1:lc]
        g_col = b_last - b_col + ig_col
        m_new = jnp.maximum(b_last + m_prev, jnp.max(g_col, axis=0, keepdims=True))
        decay = jnp.exp(b_last + m_prev - m_new)
        wg_col = jnp.exp(g_col - m_new)
        vw_ext = (v_ext.astype(F32) * wg_col).astype(BF16)
        k_t = k.astype(F32).T.astype(BF16)
        c_sc[hh] = decay * c_ext + jnp.dot(k_t, vw_ext, preferred_element_type=F32)
        m_sc[hh] = m_new

        hn = hid * lax.rsqrt(jnp.mean(hid * hid, axis=-1, keepdims=True) + RMS_EPS)
        og = og_ref[:, hh * dv:(hh + 1) * dv].astype(F32)
        o_ref[:, hh * dv:(hh + 1) * dv] = (hn * ng_ref[:, hh * dv:(hh + 1) * dv] * jax.nn.sigmoid(og)).astype(o_ref.dtype)


def mlstm_core(qk, vo, gates_t, norm_g, n_heads, dk, dv, lc=256, hps=4):
    s = qk.shape[0]
    d = n_heads * dv
    lc = _tile(s, lc)
    hps = math.gcd(n_heads, hps)
    n_groups = n_heads // hps
    kern = functools.partial(_mlstm_kernel, lc=lc, dk=dk, dv=dv, n_heads=n_heads, hps=hps)
    return pl.pallas_call(
        kern,
        grid=(n_groups, s // lc),
        in_specs=[pl.BlockSpec((lc, hps * dk), lambda g, c: (c, g)),
                  pl.BlockSpec((lc, hps * dk), lambda g, c: (c, n_groups + g)),
                  pl.BlockSpec((lc, hps * dv), lambda g, c: (c, g)),
                  pl.BlockSpec((lc, hps * dv), lambda g, c: (c, n_groups + g)),
                  pl.BlockSpec((2 * n_heads, lc), lambda g, c: (0, c)),
                  pl.BlockSpec((1, hps * dv), lambda g, c: (0, g))],
        out_specs=pl.BlockSpec((lc, hps * dv), lambda g, c: (c, g)),
        out_shape=jax.ShapeDtypeStruct((s, d), BF16),
        scratch_shapes=[pltpu.VMEM((hps, dk, dv + LANE), F32), pltpu.VMEM((hps, 1, 1), F32)],
        compiler_params=_cparams(("parallel", "arbitrary")),
        name="mlstm_chunks",
    )(qk, qk, vo, vo, gates_t, norm_g.reshape(1, d))


def _split_bf16(x):
    hi = x.astype(BF16)
    lo = (x - hi.astype(F32)).astype(BF16)
    return hi, lo


def _router_kernel(x_ref, wt_ref, br_ref, ri_ref, rw_ref, cnt_ref, lg_sc, base_sc, *, n_exp, tm):
    step = pl.program_id(0)
    per_group = n_exp // N_GROUPS

    @pl.when(step == 0)
    def _():
        base_sc[...] = jnp.zeros(base_sc.shape, F32)

    nt = (((1,), (1,)), ((), ()))
    x_hi, x_lo = _split_bf16(x_ref[...])
    w_hi, w_lo = _split_bf16(wt_ref[...])
    lg_sc[...] = (lax.dot_general(w_hi, x_hi, nt, preferred_element_type=F32)
                  + lax.dot_general(w_hi, x_lo, nt, preferred_element_type=F32)
                  + lax.dot_general(w_lo, x_hi, nt, preferred_element_type=F32))

    logit = [lg_sc[e:e + 1, :] for e in range(n_exp)]
    mx = functools.reduce(jnp.maximum, logit)
    ex = [jnp.exp(l - mx) for l in logit]
    inv = 1.0 / functools.reduce(lambda a, b: a + b, ex)
    prob = [e * inv for e in ex]
    sel = [prob[e] + br_ref[e:e + 1, :] for e in range(n_exp)]

    def top2_sum(a):
        hi01, lo01 = jnp.maximum(a[0], a[1]), jnp.minimum(a[0], a[1])
        hi23, lo23 = jnp.maximum(a[2], a[3]), jnp.minimum(a[2], a[3])
        return jnp.maximum(hi01, hi23) + jnp.maximum(jnp.minimum(hi01, hi23), jnp.maximum(lo01, lo23))

    assert per_group == 4
    gscore = [top2_sum(sel[g * per_group:(g + 1) * per_group]) for g in range(N_GROUPS)]
    best, grp = gscore[0], jnp.zeros_like(gscore[0], dtype=jnp.int32)
    for g in range(1, N_GROUPS):
        upd = gscore[g] > best
        grp = jnp.where(upd, g, grp)
        best = jnp.where(upd, gscore[g], best)

    def pick(vals, i):
        out = vals[i]
        for g in range(1, N_GROUPS):
            out = jnp.where(grp == g, vals[g * per_group + i], out)
        return out

    gs = [pick(sel, i) for i in range(per_group)]
    gp = [pick(prob, i) for i in range(per_group)]
    v1, i1, p1 = gs[0], jnp.zeros_like(grp), gp[0]
    for i in range(1, per_group):
        upd = gs[i] > v1
        v1, i1, p1 = jnp.where(upd, gs[i], v1), jnp.where(upd, i, i1), jnp.where(upd, gp[i], p1)
    v2 = jnp.full_like(v1, -jnp.inf)
    i2, p2 = jnp.zeros_like(grp), jnp.zeros_like(p1)
    for i in range(per_group):
        upd = (i1 != i) & (gs[i] > v2)
        v2, i2, p2 = jnp.where(upd, gs[i], v2), jnp.where(upd, i, i2), jnp.where(upd, gp[i], p2)
    e1 = grp * per_group + i1
    e2 = grp * per_group + i2
    wsum = p1 + p2
    rw_ref[...] = jnp.zeros(rw_ref.shape, F32)
    rw_ref[0:1, :] = p1 / wsum
    rw_ref[1:2, :] = p2 / wsum

    for e in range(n_exp):
        lg_sc[e:e + 1, :] = ((e1 == e) | (e2 == e)).astype(F32)
    mask = lg_sc[...]
    r_i = lax.broadcasted_iota(jnp.int32, (tm, tm), 0)
    c_i = lax.broadcasted_iota(jnp.int32, (tm, tm), 1)
    upper = (r_i < c_i).astype(BF16)
    lg_sc[...] = jnp.dot(mask.astype(BF16), upper, preferred_element_type=F32) + base_sc[...]
    new_base = base_sc[...] + jnp.sum(mask, axis=1, keepdims=True)
    base_sc[...] = new_base
    cnt_ref[...] = jnp.broadcast_to(new_base, cnt_ref.shape).astype(jnp.int32)
    r1 = jnp.zeros_like(p1)
    r2 = jnp.zeros_like(p1)
    for e in range(n_exp):
        excl_e = lg_sc[e:e + 1, :]
        r1 = jnp.where(e1 == e, excl_e, r1)
        r2 = jnp.where(e2 == e, excl_e, r2)
    ri_ref[...] = jnp.zeros(ri_ref.shape, jnp.int32)
    ri_ref[0:1, :] = e1
    ri_ref[1:2, :] = e2
    ri_ref[2:3, :] = r1.astype(jnp.int32)
    ri_ref[3:4, :] = r2.astype(jnp.int32)


def router(x, w_router, b_router, tm=512):
    s, d = x.shape
    n_exp = w_router.shape[1]
    tm = _tile(s, tm)
    kern = functools.partial(_router_kernel, n_exp=n_exp, tm=tm)
    return pl.pallas_call(
        kern,
        grid=(s // tm,),
        in_specs=[pl.BlockSpec((tm, d), lambda i: (i, 0)),
                  pl.BlockSpec((n_exp, d), lambda i: (0, 0)),
                  pl.BlockSpec((n_exp, 1), lambda i: (0, 0))],
        out_specs=[pl.BlockSpec((8, tm), lambda i: (0, i)),
                   pl.BlockSpec((8, tm), lambda i: (0, i)),
                   pl.BlockSpec((n_exp, LANE), lambda i: (0, 0))],
        out_shape=[jax.ShapeDtypeStruct((8, s), jnp.int32), jax.ShapeDtypeStruct((8, s), F32),
                   jax.ShapeDtypeStruct((n_exp, LANE), jnp.int32)],
        scratch_shapes=[pltpu.VMEM((n_exp, tm), F32), pltpu.VMEM((n_exp, 1), F32)],
        compiler_params=_cparams(("arbitrary",)),
        name="moe_router",
    )(x, w_router.T, b_router.reshape(n_exp, 1))


def _dispatch_kernel(dest_ref, pad_ref, x_ref, xs_ref, zero_sc, sem, *, tt, n_tok, n_exp):
    i = pl.program_id(0)
    base = i * tt

    def row_copy(src, r_src, r_dst):
        return pltpu.make_async_copy(src.at[pl.ds(r_src, 1), :], xs_ref.at[pl.ds(r_dst, 1), :], sem)

    @pl.when(i == 0)
    def _():
        zero_sc[...] = jnp.zeros(zero_sc.shape, zero_sc.dtype)
        for e in range(n_exp):
            lo, hi = pad_ref[e], pad_ref[n_exp + e]
            lax.fori_loop(lo, hi, lambda r, c: (row_copy(zero_sc, 0, r).start(), c)[1], 0)
        for e in range(n_exp):
            lo, hi = pad_ref[e], pad_ref[n_exp + e]
            lax.fori_loop(lo, hi, lambda r, c: (row_copy(zero_sc, 0, r).wait(), c)[1], 0)

    def start_tok(r, c):
        row_copy(x_ref, r, dest_ref[base + r]).start()
        row_copy(x_ref, r, dest_ref[n_tok + base + r]).start()
        return c

    def wait_tok(r, c):
        row_copy(x_ref, r, dest_ref[base + r]).wait()
        row_copy(x_ref, r, dest_ref[n_tok + base + r]).wait()
        return c

    lax.fori_loop(0, tt, start_tok, 0)
    lax.fori_loop(0, tt, wait_tok, 0)


def dispatch(x, dest, pad_bounds, n_slots, n_exp, tt=256):
    s, d = x.shape
    tt = _tile(s, tt)
    kern = functools.partial(_dispatch_kernel, tt=tt, n_tok=s, n_exp=n_exp)
    return pl.pallas_call(
        kern,
        grid_spec=pltpu.PrefetchScalarGridSpec(
            num_scalar_prefetch=2,
            grid=(s // tt,),
            in_specs=[pl.BlockSpec((tt, d), lambda i, dr, pr: (i, 0))],
            out_specs=pl.BlockSpec(memory_space=pl.ANY),
            scratch_shapes=[pltpu.VMEM((8, d), x.dtype), pltpu.SemaphoreType.DMA],
        ),
        out_shape=jax.ShapeDtypeStruct((n_slots, d), x.dtype),
        compiler_params=_cparams(("arbitrary",)),
        name="moe_dispatch",
    )(dest, pad_bounds, x)


def _cast_kernel(x_ref, o_ref):
    o_ref[...] = x_ref[...].astype(o_ref.dtype)


def cast_layer_weights(w4, layer, dtype=BF16):
    _, n_exp, a, b = w4.shape
    return pl.pallas_call(
        _cast_kernel,
        grid=(n_exp,),
        in_specs=[pl.BlockSpec((None, None, a, b), lambda e: (layer, e, 0, 0))],
        out_specs=pl.BlockSpec((None, a, b), lambda e: (e, 0, 0)),
        out_shape=jax.ShapeDtypeStruct((n_exp, a, b), dtype),
        compiler_params=_cparams(("parallel",)),
        name="cast_weights",
    )(w4)


def _expert_kernel(te_ref, ts_ref, nv_ref, xs_ref, wg_ref, wu_ref, wd_ref, ys_ref):
    @pl.when(pl.program_id(0) < nv_ref[0])
    def _():
        x = xs_ref[...].astype(BF16)
        g = jnp.dot(x, wg_ref[...], preferred_element_type=F32)
        u = jnp.dot(x, wu_ref[...], preferred_element_type=F32)
        hmid = (g * jax.nn.sigmoid(g) * u).astype(BF16)
        ys_ref[...] = jnp.dot(hmid, wd_ref[...], preferred_element_type=F32).astype(ys_ref.dtype)


def expert_mlp(xs, w_gate, w_up, w_down, tile_expert, tile_src, n_valid, tm):
    n_slots, d = xs.shape
    _, _, f = w_gate.shape
    n_tiles = n_slots // tm
    return pl.pallas_call(
        _expert_kernel,
        grid_spec=pltpu.PrefetchScalarGridSpec(
            num_scalar_prefetch=3,
            grid=(n_tiles,),
            in_specs=[pl.BlockSpec((tm, d), lambda t, te, ts, nv: (ts[t], 0)),
                      pl.BlockSpec((None, d, f), lambda t, te, ts, nv: (te[t], 0, 0)),
                      pl.BlockSpec((None, d, f), lambda t, te, ts, nv: (te[t], 0, 0)),
                      pl.BlockSpec((None, f, d), lambda t, te, ts, nv: (te[t], 0, 0))],
            out_specs=pl.BlockSpec((tm, d), lambda t, te, ts, nv: (ts[t], 0)),
        ),
        out_shape=jax.ShapeDtypeStruct((n_slots, d), F32),
        compiler_params=_cparams(("arbitrary",)),
        name="moe_experts",
    )(tile_expert, tile_src, n_valid, xs, w_gate, w_up, w_down)


def _combine_kernel(dest_ref, x_ref, w_ref, g_ref, b_ref, ys_ref, of_ref, ob_ref, buf, sem, *, tt, n_tok, alpha):
    i = pl.program_id(0)
    n_steps = pl.num_programs(0)

    def row_copy(step, slot, kk, r):
        tok = step * tt + r
        return pltpu.make_async_copy(ys_ref.at[pl.ds(dest_ref[kk * n_tok + tok], 1), :],
                                     buf.at[slot, kk, pl.ds(r, 1), :], sem.at[slot])

    def start_tile(step, slot):
        def body(r, c):
            row_copy(step, slot, 0, r).start()
            row_copy(step, slot, 1, r).start()
            return c
        lax.fori_loop(0, tt, body, 0)

    def wait_tile(step, slot):
        def body(r, c):
            row_copy(step, slot, 0, r).wait()
            row_copy(step, slot, 1, r).wait()
            return c
        lax.fori_loop(0, tt, body, 0)

    slot = i % 2

    @pl.when(i == 0)
    def _():
        start_tile(0, 0)

    @pl.when(i + 1 < n_steps)
    def _():
        start_tile(i + 1, 1 - slot)

    wait_tile(i, slot)
    w = w_ref[...]
    y = w[:, 0:1] * buf[slot, 0] + w[:, 1:2] * buf[slot, 1]
    out = _layer_norm_rows(alpha * x_ref[...] + y, g_ref[...], b_ref[...])
    of_ref[...] = out
    ob_ref[...] = out.astype(BF16)


def combine_layer_norm(x, ys, dest, w_tok, g, b, alpha, tt=256):
    s, d = x.shape
    tt = _tile(s, tt)
    kern = functools.partial(_combine_kernel, tt=tt, n_tok=s, alpha=alpha)
    row = lambda i, dr: (i, 0)
    fixed = lambda i, dr: (0, 0)
    return pl.pallas_call(
        kern,
        grid_spec=pltpu.PrefetchScalarGridSpec(
            num_scalar_prefetch=1,
            grid=(s // tt,),
            in_specs=[pl.BlockSpec((tt, d), row),
                      pl.BlockSpec((tt, TOP_K), row),
                      pl.BlockSpec((1, d), fixed),
                      pl.BlockSpec((1, d), fixed),
                      pl.BlockSpec(memory_space=pl.ANY)],
            out_specs=[pl.BlockSpec((tt, d), row), pl.BlockSpec((tt, d), row)],
            scratch_shapes=[pltpu.VMEM((2, TOP_K, tt, d), F32), pltpu.SemaphoreType.DMA((2,))],
        ),
        out_shape=[jax.ShapeDtypeStruct((s, d), F32), jax.ShapeDtypeStruct((s, d), BF16)],
        compiler_params=_cparams(("arbitrary",)),
        name="moe_combine_layer_norm",
    )(dest, x, w_tok, g.reshape(1, d), b.reshape(1, d), ys)


def moe_block(x_f32, w_router, b_router, w_gate, w_up, w_down, ln_g, ln_b, alpha, tm=256):
    s, d = x_f32.shape
    n_exp = w_router.shape[1]
    tm = _tile(s, tm)
    route_i, route_w, counts = router(x_f32, w_router, b_router)
    counts = counts[:, 0]
    padded = ((counts + tm - 1) // tm) * tm
    ends = jnp.cumsum(padded)
    offsets = ends - padded
    e1, e2, r1, r2 = route_i[0], route_i[1], route_i[2], route_i[3]
    dest = jnp.concatenate([offsets[e1] + r1, offsets[e2] + r2]).astype(jnp.int32)
    pad_bounds = jnp.concatenate([offsets + counts, ends]).astype(jnp.int32)
    n_slots = TOP_K * s + n_exp * tm
    n_tiles = n_slots // tm
    n_valid = (ends[-1] // tm).astype(jnp.int32)
    tile_start = jnp.arange(n_tiles, dtype=jnp.int32) * tm
    tile_expert = jnp.minimum(jnp.sum(tile_start[:, None] >= ends[None, :], axis=1), n_exp - 1).astype(jnp.int32)
    tile_src = jnp.minimum(jnp.arange(n_tiles, dtype=jnp.int32), n_valid - 1)
    tile_expert = tile_expert[tile_src]

    xs = dispatch(x_f32, dest, pad_bounds, n_slots, n_exp)
    ys = expert_mlp(xs, w_gate, w_up, w_down, tile_expert, tile_src, n_valid.reshape(1), tm)
    w_tok = jnp.stack([route_w[0], route_w[1]], axis=1)
    return combine_layer_norm(x_f32, ys, dest, w_tok, ln_g, ln_b, alpha)


def kernel(x, attn_w_in, attn_lambda, attn_norm_g, attn_w_out, mlstm_w_in, mlstm_gate_b, mlstm_conv_w,
           mlstm_norm_g, mlstm_w_out, ln_mix_g, ln_mix_b, ln_ffn_g, ln_ffn_b, w_router, b_router,
           moe_w_gate, moe_w_up, moe_w_down):
    batch, s, d = x.shape
    assert batch == 1
    depth = ln_mix_g.shape[0]
    alpha = (2 * depth) ** 0.25
    dh = attn_lambda.shape[-1]
    da_heads = d // (2 * dh)
    ml_heads = mlstm_gate_b.shape[-1] // 2
    qk_width = mlstm_conv_w.shape[-1] // 2
    dk = qk_width // ml_heads
    dv = d // ml_heads

    mlstm_wt = jnp.swapaxes(mlstm_w_in, 1, 2)
    xf = x.reshape(s, d)
    xb = xf.astype(BF16)
    for i in range(depth):
        j = i // 2
        if i % 2 == 0:
            lambda_init = 0.8 - 0.6 * math.exp(-0.3 * i)
            lv = attn_lambda[j].astype(F32)
            lam = jnp.exp(jnp.sum(lv[0] * lv[1])) - jnp.exp(jnp.sum(lv[2] * lv[3])) + lambda_init
            proj = matmul(xb, attn_w_in, j, 0, 3 * d, BF16)
            o = diff_attention_core(proj, lam, attn_norm_g[j], lambda_init, da_heads, dh)
            mix = matmul(o, attn_w_out, j, 0, d, BF16)
        else:
            qk_pre = matmul(xb, mlstm_wt, j, 0, 2 * qk_width, F32, w_transposed=True)
            vo = matmul(xb, mlstm_wt, j, 2 * qk_width, 2 * d, BF16, w_transposed=True)
            gates = mlstm_gates(xb, mlstm_wt, j, 2 * qk_width + 2 * d, mlstm_gate_b[j], ml_heads)
            qk = causal_conv_silu(qk_pre, mlstm_conv_w[j], qk_width, dk ** -0.5)
            hcore = mlstm_core(qk, vo, gates.T, mlstm_norm_g[j], ml_heads, dk, dv)
            mix = matmul(hcore, mlstm_w_out, j, 0, d, BF16)
        xf, xb = residual_layer_norm(xf, mix, ln_mix_g[i], ln_mix_b[i], alpha)
        xf, xb = moe_block(xf, w_router, b_router, cast_layer_weights(moe_w_gate, i), cast_layer_weights(moe_w_up, i),
                           cast_layer_weights(moe_w_down, i), ln_ffn_g[i], ln_ffn_b[i], alpha)
    return xf.reshape(batch, s, d)
```

```python
import functools
import math

import jax
import jax.numpy as jnp
from jax import lax
from jax.experimental import pallas as pl
from jax.experimental.pallas import tpu as pltpu

F32 = jnp.float32
BF16 = jnp.bfloat16

LN_EPS = 1e-5
RMS_EPS = 1e-6
N_GROUPS = 4
TOP_K = 2
LOG2E = math.log2(math.e)

V7X_VMEM_BYTES = 64 * 1024 * 1024
VMEM_LIMIT = V7X_VMEM_BYTES - 8 * 1024 * 1024
LANE = 128


def _cparams(sem):
    return pltpu.CompilerParams(dimension_semantics=sem, vmem_limit_bytes=VMEM_LIMIT)


def _tile(n, pref):
    t = min(n, pref)
    while n % t:
        t //= 2
    return t


NT_DIMS = (((1,), (1,)), ((), ()))


def _mm_kernel(a_ref, w_ref, o_ref, *, w_transposed):
    a = a_ref[...].astype(BF16)
    w = w_ref[...].astype(BF16)
    if w_transposed:
        acc = lax.dot_general(a, w, NT_DIMS, preferred_element_type=F32)
    else:
        acc = jnp.dot(a, w, preferred_element_type=F32)
    o_ref[...] = acc.astype(o_ref.dtype)


def matmul(a, w3, layer, col0, n, out_dtype, tm=1024, tn=512, w_transposed=False):
    m, k = a.shape
    tm = _tile(m, tm)
    tn = _tile(n, tn)
    assert col0 % tn == 0
    cb = col0 // tn
    if w_transposed:
        w_spec = pl.BlockSpec((None, tn, k), lambda i, j: (layer, cb + j, 0))
    else:
        w_spec = pl.BlockSpec((None, k, tn), lambda i, j: (layer, 0, cb + j))
    return pl.pallas_call(
        functools.partial(_mm_kernel, w_transposed=w_transposed),
        grid=(m // tm, n // tn),
        in_specs=[pl.BlockSpec((tm, k), lambda i, j: (i, 0)), w_spec],
        out_specs=pl.BlockSpec((tm, tn), lambda i, j: (i, j)),
        out_shape=jax.ShapeDtypeStruct((m, n), out_dtype),
        compiler_params=_cparams(("parallel", "parallel")),
        name="proj_matmul",
    )(a, w3)


def _layer_norm_rows(z, g, b):
    mu = jnp.mean(z, axis=-1, keepdims=True)
    zc = z - mu
    var = jnp.mean(zc * zc, axis=-1, keepdims=True)
    return zc * lax.rsqrt(var + LN_EPS) * g + b


def _ln_kernel(x_ref, y_ref, g_ref, b_ref, of_ref, ob_ref, *, alpha):
    z = alpha * x_ref[...] + y_ref[...].astype(F32)
    out = _layer_norm_rows(z, g_ref[...], b_ref[...])
    of_ref[...] = out
    ob_ref[...] = out.astype(BF16)


def residual_layer_norm(x, y, g, b, alpha, tr=256):
    s, d = x.shape
    tr = _tile(s, tr)
    row = pl.BlockSpec((tr, d), lambda i: (i, 0))
    vec = pl.BlockSpec((1, d), lambda i: (0, 0))
    return pl.pallas_call(
        functools.partial(_ln_kernel, alpha=alpha),
        grid=(s // tr,),
        in_specs=[row, row, vec, vec],
        out_specs=[row, row],
        out_shape=[jax.ShapeDtypeStruct((s, d), F32), jax.ShapeDtypeStruct((s, d), BF16)],
        compiler_params=_cparams(("parallel",)),
        name="residual_layer_norm",
    )(x, y, g.reshape(1, d), b.reshape(1, d))


def _transpose_kernel(x_ref, o_ref):
    o_ref[...] = x_ref[...].astype(F32).T.astype(o_ref.dtype)


def transpose_cols(x, col0, n, t=512):
    s = x.shape[0]
    t = _tile(s, t)
    tc = _tile(n, t)
    assert col0 % tc == 0
    cb = col0 // tc
    return pl.pallas_call(
        _transpose_kernel,
        grid=(s // t, n // tc),
        in_specs=[pl.BlockSpec((t, tc), lambda i, j: (i, cb + j))],
        out_specs=pl.BlockSpec((tc, t), lambda i, j: (j, i)),
        out_shape=jax.ShapeDtypeStruct((n, s), x.dtype),
        compiler_params=_cparams(("parallel", "parallel")),
        name="transpose_cols",
    )(x)


def _attn_kernel(lam_ref, q_ref, k_ref, vt_ref, g_ref, o_ref, m_sc, l_sc, acc_sc, sa_sc, sb_sc, *,
                 t, dh, q_scale, out_scale):
    n = pl.program_id(1)
    nt_dims = (((1,), (1,)), ((), ()))
    qb = (q_ref[...].astype(F32) * q_scale).astype(BF16)
    m_sc[...] = jnp.full(m_sc.shape, -jnp.inf, F32)
    l_sc[...] = jnp.zeros(l_sc.shape, F32)
    acc_sc[...] = jnp.zeros(acc_sc.shape, F32)

    def scores(j, dst):
        off = pl.multiple_of(j * t, t)
        kb = k_ref[pl.ds(off, t), :]
        for c in range(2):
            dst[c] = lax.dot_general(kb[:, c * dh:(c + 1) * dh], qb[:, c * dh:(c + 1) * dh], nt_dims,
                                     preferred_element_type=F32)

    def accumulate(j, src, masked):
        off = pl.multiple_of(j * t, t)
        vtb = vt_ref[:, pl.ds(off, t)]
        for c in range(2):
            s = src[c]
            if masked:
                krow = lax.broadcasted_iota(jnp.int32, (t, t), 0)
                qcol = lax.broadcasted_iota(jnp.int32, (t, t), 1)
                s = jnp.where(krow <= qcol, s, -jnp.inf)
            m_prev = m_sc[c]
            m_new = jnp.maximum(m_prev, jnp.max(s, axis=0, keepdims=True))
            a = jnp.exp2(m_prev - m_new)
            p = jnp.exp2(s - m_new)
            l_sc[c] = a * l_sc[c] + jnp.sum(p, axis=0, keepdims=True)
            acc_sc[c] = a * acc_sc[c] + jnp.dot(vtb, p.astype(BF16), preferred_element_type=F32)
            m_sc[c] = m_new

    scores(0, sa_sc)

    def pair(pp, carry):
        j = 2 * pp
        scores(j + 1, sb_sc)
        accumulate(j, sa_sc, False)
        scores(j + 2, sa_sc)
        accumulate(j + 1, sb_sc, False)
        return carry

    lax.fori_loop(0, n // 2, pair, 0)

    @pl.when(n % 2 == 0)
    def _():
        accumulate(n, sa_sc, True)

    @pl.when(n % 2 == 1)
    def _():
        scores(n, sb_sc)
        accumulate(n - 1, sa_sc, False)
        accumulate(n, sb_sc, True)

    lam = lam_ref[0, 0]
    o = acc_sc[0] / l_sc[0] - lam * (acc_sc[1] / l_sc[1])
    o = o * lax.rsqrt(jnp.mean(o * o, axis=0, keepdims=True) + RMS_EPS)
    o = o * (g_ref[...] * out_scale)
    o_ref[...] = o.T.astype(o_ref.dtype)


def diff_attention_core(proj, lam, norm_g, lambda_init, n_heads, dh, t=512):
    s, d3 = proj.shape
    hd = 2 * dh
    d = n_heads * hd
    t = _tile(s, t)
    vt = transpose_cols(proj, 2 * d, d)
    kern = functools.partial(_attn_kernel, t=t, dh=dh, q_scale=(dh ** -0.5) * LOG2E,
                             out_scale=1.0 - lambda_init)
    return pl.pallas_call(
        kern,
        grid=(n_heads, s // t),
        in_specs=[pl.BlockSpec(memory_space=pltpu.SMEM),
                  pl.BlockSpec((t, hd), lambda h, i: (i, h)),
                  pl.BlockSpec((s, hd), lambda h, i: (0, n_heads + h)),
                  pl.BlockSpec((hd, s), lambda h, i: (h, 0)),
                  pl.BlockSpec((hd, 1), lambda h, i: (0, 0))],
        out_specs=pl.BlockSpec((t, hd), lambda h, i: (i, h)),
        out_shape=jax.ShapeDtypeStruct((s, d), BF16),
        scratch_shapes=[pltpu.VMEM((2, 1, t), F32), pltpu.VMEM((2, 1, t), F32), pltpu.VMEM((2, hd, t), F32),
                        pltpu.VMEM((2, t, t), F32), pltpu.VMEM((2, t, t), F32)],
        compiler_params=_cparams(("parallel", "arbitrary")),
        name="diff_attention",
    )(lam.reshape(1, 1), proj, proj, vt, norm_g.reshape(hd, 1))


def _gates_kernel(a_ref, w_ref, b_ref, o_ref, *, n_heads):
    z = lax.dot_general(a_ref[...], w_ref[...].astype(BF16), NT_DIMS, preferred_element_type=F32) + b_ref[...]
    log_sig = jnp.minimum(z, 0.0) - jnp.log(1.0 + jnp.exp(-jnp.abs(z)))
    col = lax.broadcasted_iota(jnp.int32, z.shape, 1)
    o_ref[...] = jnp.where(col >= n_heads, log_sig, z)


def mlstm_gates(a, wt3, layer, row0, bias, n_heads, tm=1024):
    m, k = a.shape
    n = 2 * n_heads
    assert row0 % n == 0
    rb = row0 // n
    tm = _tile(m, tm)
    return pl.pallas_call(
        functools.partial(_gates_kernel, n_heads=n_heads),
        grid=(m // tm,),
        in_specs=[pl.BlockSpec((tm, k), lambda i: (i, 0)),
                  pl.BlockSpec((None, n, k), lambda i: (layer, rb, 0)),
                  pl.BlockSpec((1, n), lambda i: (0, 0))],
        out_specs=pl.BlockSpec((tm, n), lambda i: (i, 0)),
        out_shape=jax.ShapeDtypeStruct((m, n), F32),
        compiler_params=_cparams(("parallel",)),
        name="mlstm_gates",
    )(a, wt3, bias.reshape(1, n))


def _conv_kernel(u_ref, w_ref, o_ref, tail_sc, *, n_taps, q_tiles, q_scale):
    j = pl.program_id(0)
    ti = pl.program_id(1)

    @pl.when(ti == 0)
    def _():
        tail_sc[...] = jnp.zeros(tail_sc.shape, F32)

    u = u_ref[...]
    tr = u.shape[0]
    tail = tail_sc[...]
    row8 = lax.broadcasted_iota(jnp.int32, tail.shape, 0)
    acc = u * w_ref[n_taps - 1:n_taps, :]
    for sh in range(1, n_taps):
        rolled = pltpu.roll(u, sh, 0)
        head = jnp.where(row8 < sh, pltpu.roll(tail, sh, 0), rolled[:8])
        shifted = jnp.concatenate([head, rolled[8:]], axis=0)
        acc = acc + shifted * w_ref[n_taps - 1 - sh:n_taps - sh, :]
    tail_sc[...] = u[tr - 8:, :]
    y = acc * jax.nn.sigmoid(acc)
    y = y * jnp.where(j < q_tiles, q_scale, 1.0)
    o_ref[...] = y.astype(o_ref.dtype)


def causal_conv_silu(u, w, qk_width, q_scale, tr=512, tc=512):
    s, c = u.shape
    n_taps = w.shape[0]
    tr = _tile(s, tr)
    tc = _tile(qk_width, tc)
    kern = functools.partial(_conv_kernel, n_taps=n_taps, q_tiles=qk_width // tc, q_scale=q_scale)
    return pl.pallas_call(
        kern,
        grid=(c // tc, s // tr),
        in_specs=[pl.BlockSpec((tr, tc), lambda j, i: (i, j)),
                  pl.BlockSpec((n_taps, tc), lambda j, i: (0, j))],
        out_specs=pl.BlockSpec((tr, tc), lambda j, i: (i, j)),
        out_shape=jax.ShapeDtypeStruct((s, c), BF16),
        scratch_shapes=[pltpu.VMEM((8, tc), F32)],
        compiler_params=_cparams(("parallel", "arbitrary")),
        name="causal_conv_silu",
    )(u, w)


def _mlstm_kernel(q_ref, k_ref, v_ref, og_ref, gt_ref, ng_ref, o_ref, c_sc, m_sc, *, lc, dk, dv, n_heads, hps):
    hg = pl.program_id(0)
    ci = pl.program_id(1)

    @pl.when(ci == 0)
    def _():
        c_sc[...] = jnp.zeros(c_sc.shape, F32)
        m_sc[...] = jnp.zeros(m_sc.shape, F32)

    lane8 = lax.broadcasted_iota(jnp.int32, (8, lc), 1)
    half = lc // 2
    rows = lax.broadcasted_iota(jnp.int32, (lc, lc), 0)
    cols = lax.broadcasted_iota(jnp.int32, (lc, lc), 1)
    ones_col = (lax.broadcasted_iota(jnp.int32, (lc, LANE), 1) == 0).astype(BF16)

    for hh in range(hps):
        h = hg * hps + hh
        ig_row = gt_ref[pl.ds(h, 1), :]
        lf_row = gt_ref[pl.ds(h + n_heads, 1), :]

        b8 = jnp.broadcast_to(lf_row, (8, lc))
        d = 1
        while d < lc:
            b8 = b8 + jnp.where(lane8 >= d, pltpu.roll(b8, d, 1), 0.0)
            d *= 2
        b_row = b8[0:1, :]

        stacked = jnp.where(rows < half, jnp.broadcast_to(b_row, (lc, lc)), jnp.broadcast_to(ig_row, (lc, lc)))
        stacked_t = stacked.T
        b_col = stacked_t[:, 0:1]
        ig_col = stacked_t[:, half:half + 1]

        m_prev = m_sc[hh]
        dlog = jnp.where(cols <= rows, b_col - b_row + ig_row, -jnp.inf)
        inter = b_col + m_prev
        m_out = jnp.maximum(inter, jnp.max(dlog, axis=-1, keepdims=True))
        dw = jnp.exp(dlog - m_out)
        inter_w = jnp.exp(inter - m_out)

        q = q_ref[:, hh * dk:(hh + 1) * dk]
        k = k_ref[:, hh * dk:(hh + 1) * dk]
        v_ext = jnp.concatenate([v_ref[:, hh * dv:(hh + 1) * dv], ones_col], axis=1)

        scores = lax.dot_general(q, k, NT_DIMS, preferred_element_type=F32) * dw
        c_ext = c_sc[hh]
        num_ext = inter_w * jnp.dot(q, c_ext.astype(BF16), preferred_element_type=F32) \
            + jnp.dot(scores.astype(BF16), v_ext, preferred_element_type=F32)
        num = num_ext[:, :dv]
        den = num_ext[:, dv:dv + 1]
        hid = num / jnp.maximum(jnp.abs(den), jnp.exp(-m_out))

        b_last = b_row[:, lc - 1:lc]
        g_col = b_last - b_col + ig_col
        m_new = jnp.maximum(b_last + m_prev, jnp.max(g_col, axis=0, keepdims=True))
        decay = jnp.exp(b_last + m_prev - m_new)
        wg_col = jnp.exp(g_col - m_new)
        vw_ext = (v_ext.astype(F32) * wg_col).astype(BF16)
        k_t = k.astype(F32).T.astype(BF16)
        c_sc[hh] = decay * c_ext + jnp.dot(k_t, vw_ext, preferred_element_type=F32)
        m_sc[hh] = m_new

        hn = hid * lax.rsqrt(jnp.mean(hid * hid, axis=-1, keepdims=True) + RMS_EPS)
        og = og_ref[:, hh * dv:(hh + 1) * dv].astype(F32)
        o_ref[:, hh * dv:(hh + 1) * dv] = (hn * ng_ref[:, hh * dv:(hh + 1) * dv] * jax.nn.sigmoid(og)).astype(o_ref.dtype)


def mlstm_core(qk, vo, gates_t, norm_g, n_heads, dk, dv, lc=256, hps=4):
    s = qk.shape[0]
    d = n_heads * dv
    lc = _tile(s, lc)
    hps = math.gcd(n_heads, hps)
    n_groups = n_heads // hps
    kern = functools.partial(_mlstm_kernel, lc=lc, dk=dk, dv=dv, n_heads=n_heads, hps=hps)
    return pl.pallas_call(
        kern,
        grid=(n_groups, s // lc),
        in_specs=[pl.BlockSpec((lc, hps * dk), lambda g, c: (c, g)),
                  pl.BlockSpec((lc, hps * dk), lambda g, c: (c, n_groups + g)),
                  pl.BlockSpec((lc, hps * dv), lambda g, c: (c, g)),
                  pl.BlockSpec((lc, hps * dv), lambda g, c: (c, n_groups + g)),
                  pl.BlockSpec((2 * n_heads, lc), lambda g, c: (0, c)),
                  pl.BlockSpec((1, hps * dv), lambda g, c: (0, g))],
        out_specs=pl.BlockSpec((lc, hps * dv), lambda g, c: (c, g)),
        out_shape=jax.ShapeDtypeStruct((s, d), BF16),
        scratch_shapes=[pltpu.VMEM((hps, dk, dv + LANE), F32), pltpu.VMEM((hps, 1, 1), F32)],
        compiler_params=_cparams(("parallel", "arbitrary")),
        name="mlstm_chunks",
    )(qk, qk, vo, vo, gates_t, norm_g.reshape(1, d))


def _split_bf16(x):
    hi = x.astype(BF16)
    lo = (x - hi.astype(F32)).astype(BF16)
    return hi, lo


def _router_kernel(x_ref, wt_ref, br_ref, ri_ref, rw_ref, cnt_ref, lg_sc, base_sc, *, n_exp, tm):
    step = pl.program_id(0)
    per_group = n_exp // N_GROUPS

    @pl.when(step == 0)
    def _():
        base_sc[...] = jnp.zeros(base_sc.shape, F32)

    nt = (((1,), (1,)), ((), ()))
    x_hi, x_lo = _split_bf16(x_ref[...])
    w_hi, w_lo = _split_bf16(wt_ref[...])
    lg_sc[...] = (lax.dot_general(w_hi, x_hi, nt, preferred_element_type=F32)
                  + lax.dot_general(w_hi, x_lo, nt, preferred_element_type=F32)
                  + lax.dot_general(w_lo, x_hi, nt, preferred_element_type=F32))

    logit = [lg_sc[e:e + 1, :] for e in range(n_exp)]
    mx = functools.reduce(jnp.maximum, logit)
    ex = [jnp.exp(l - mx) for l in logit]
    inv = 1.0 / functools.reduce(lambda a, b: a + b, ex)
    prob = [e * inv for e in ex]
    sel = [prob[e] + br_ref[e:e + 1, :] for e in range(n_exp)]

    def top2_sum(a):
        hi01, lo01 = jnp.maximum(a[0], a[1]), jnp.minimum(a[0], a[1])
        hi23, lo23 = jnp.maximum(a[2], a[3]), jnp.minimum(a[2], a[3])
        return jnp.maximum(hi01, hi23) + jnp.maximum(jnp.minimum(hi01, hi23), jnp.maximum(lo01, lo23))

    assert per_group == 4
    gscore = [top2_sum(sel[g * per_group:(g + 1) * per_group]) for g in range(N_GROUPS)]
    best, grp = gscore[0], jnp.zeros_like(gscore[0], dtype=jnp.int32)
    for g in range(1, N_GROUPS):
        upd = gscore[g] > best
        grp = jnp.where(upd, g, grp)
        best = jnp.where(upd, gscore[g], best)

    def pick(vals, i):
        out = vals[i]
        for g in range(1, N_GROUPS):
            out = jnp.where(grp == g, vals[g * per_group + i], out)
        return out

    gs = [pick(sel, i) for i in range(per_group)]
    gp = [pick(prob, i) for i in range(per_group)]
    v1, i1, p1 = gs[0], jnp.zeros_like(grp), gp[0]
    for i in range(1, per_group):
        upd = gs[i] > v1
        v1, i1, p1 = jnp.where(upd, gs[i], v1), jnp.where(upd, i, i1), jnp.where(upd, gp[i], p1)
    v2 = jnp.full_like(v1, -jnp.inf)
    i2, p2 = jnp.zeros_like(grp), jnp.zeros_like(p1)
    for i in range(per_group):
        upd = (i1 != i) & (gs[i] > v2)
        v2, i2, p2 = jnp.where(upd, gs[i], v2), jnp.where(upd, i, i2), jnp.where(upd, gp[i], p2)
    e1 = grp * per_group + i1
    e2 = grp * per_group + i2
    wsum = p1 + p2
    rw_ref[...] = jnp.zeros(rw_ref.shape, F32)
    rw_ref[0:1, :] = p1 / wsum
    rw_ref[1:2, :] = p2 / wsum

    for e in range(n_exp):
        lg_sc[e:e + 1, :] = ((e1 == e) | (e2 == e)).astype(F32)
    mask = lg_sc[...]
    r_i = lax.broadcasted_iota(jnp.int32, (tm, tm), 0)
    c_i = lax.broadcasted_iota(jnp.int32, (tm, tm), 1)
    upper = (r_i < c_i).astype(BF16)
    lg_sc[...] = jnp.dot(mask.astype(BF16), upper, preferred_element_type=F32) + base_sc[...]
    new_base = base_sc[...] + jnp.sum(mask, axis=1, keepdims=True)
    base_sc[...] = new_base
    cnt_ref[...] = jnp.broadcast_to(new_base, cnt_ref.shape).astype(jnp.int32)
    r1 = jnp.zeros_like(p1)
    r2 = jnp.zeros_like(p1)
    for e in range(n_exp):
        excl_e = lg_sc[e:e + 1, :]
        r1 = jnp.where(e1 == e, excl_e, r1)
        r2 = jnp.where(e2 == e, excl_e, r2)
    ri_ref[...] = jnp.zeros(ri_ref.shape, jnp.int32)
    ri_ref[0:1, :] = e1
    ri_ref[1:2, :] = e2
    ri_ref[2:3, :] = r1.astype(jnp.int32)
    ri_ref[3:4, :] = r2.astype(jnp.int32)


def router(x, w_router, b_router, tm=512):
    s, d = x.shape
    n_exp = w_router.shape[1]
    tm = _tile(s, tm)
    kern = functools.partial(_router_kernel, n_exp=n_exp, tm=tm)
    return pl.pallas_call(
        kern,
        grid=(s // tm,),
        in_specs=[pl.BlockSpec((tm, d), lambda i: (i, 0)),
                  pl.BlockSpec((n_exp, d), lambda i: (0, 0)),
                  pl.BlockSpec((n_exp, 1), lambda i: (0, 0))],
        out_specs=[pl.BlockSpec((8, tm), lambda i: (0, i)),
                   pl.BlockSpec((8, tm), lambda i: (0, i)),
                   pl.BlockSpec((n_exp, LANE), lambda i: (0, 0))],
        out_shape=[jax.ShapeDtypeStruct((8, s), jnp.int32), jax.ShapeDtypeStruct((8, s), F32),
                   jax.ShapeDtypeStruct((n_exp, LANE), jnp.int32)],
        scratch_shapes=[pltpu.VMEM((n_exp, tm), F32), pltpu.VMEM((n_exp, 1), F32)],
        compiler_params=_cparams(("arbitrary",)),
        name="moe_router",
    )(x, w_router.T, b_router.reshape(n_exp, 1))


def _pack_halves(x):
    half = x.shape[1] // 2
    lo = lax.bitcast_convert_type(x[:, :half].astype(BF16).astype(F32), jnp.uint32) >> 16
    hi = lax.bitcast_convert_type(x[:, half:].astype(BF16).astype(F32), jnp.uint32) & jnp.uint32(0xFFFF0000)
    return lo | hi


def _unpack_halves(w):
    lo = lax.bitcast_convert_type(w << 16, F32)
    hi = lax.bitcast_convert_type(w & jnp.uint32(0xFFFF0000), F32)
    return lo, hi


ROW_DMA_UNROLL = 8


def _dispatch_kernel(dest_ref, pad_ref, x_ref, xs_ref, pk_sc, zero_sc, sem, *, tt, n_tok, n_exp):
    i = pl.program_id(0)
    base = i * tt
    pk_sc[...] = _pack_halves(x_ref[...])

    def row_copy(src, r_src, r_dst):
        return pltpu.make_async_copy(src.at[pl.ds(r_src, 1), :], xs_ref.at[pl.ds(r_dst, 1), :], sem)

    @pl.when(i == 0)
    def _():
        zero_sc[...] = jnp.zeros(zero_sc.shape, zero_sc.dtype)
        for e in range(n_exp):
            lo, hi = pad_ref[e], pad_ref[n_exp + e]
            lax.fori_loop(lo, hi, lambda r, c: (row_copy(zero_sc, 0, r).start(), c)[1], 0)
        for e in range(n_exp):
            lo, hi = pad_ref[e], pad_ref[n_exp + e]
            lax.fori_loop(lo, hi, lambda r, c: (row_copy(zero_sc, 0, r).wait(), c)[1], 0)

    def start_tok(r, c):
        row_copy(pk_sc, r, dest_ref[base + r]).start()
        row_copy(pk_sc, r, dest_ref[n_tok + base + r]).start()
        return c

    def wait_tok(r, c):
        row_copy(pk_sc, r, dest_ref[base + r]).wait()
        row_copy(pk_sc, r, dest_ref[n_tok + base + r]).wait()
        return c

    lax.fori_loop(0, tt, start_tok, 0, unroll=ROW_DMA_UNROLL)
    lax.fori_loop(0, tt, wait_tok, 0, unroll=ROW_DMA_UNROLL)


def dispatch(x, dest, pad_bounds, n_slots, n_exp, tt=256):
    s, d = x.shape
    tt = _tile(s, tt)
    kern = functools.partial(_dispatch_kernel, tt=tt, n_tok=s, n_exp=n_exp)
    return pl.pallas_call(
        kern,
        grid_spec=pltpu.PrefetchScalarGridSpec(
            num_scalar_prefetch=2,
            grid=(s // tt,),
            in_specs=[pl.BlockSpec((tt, d), lambda i, dr, pr: (i, 0))],
            out_specs=pl.BlockSpec(memory_space=pl.ANY),
            scratch_shapes=[pltpu.VMEM((tt, d // 2), jnp.uint32), pltpu.VMEM((8, d // 2), jnp.uint32),
                            pltpu.SemaphoreType.DMA],
        ),
        out_shape=jax.ShapeDtypeStruct((n_slots, d // 2), jnp.uint32),
        compiler_params=_cparams(("arbitrary",)),
        name="moe_dispatch",
    )(dest, pad_bounds, x)


def _cast_kernel(x_ref, o_ref):
    o_ref[...] = x_ref[...].astype(o_ref.dtype)


def cast_layer_weights(w4, layer, dtype=BF16):
    _, n_exp, a, b = w4.shape
    return pl.pallas_call(
        _cast_kernel,
        grid=(n_exp,),
        in_specs=[pl.BlockSpec((None, None, a, b), lambda e: (layer, e, 0, 0))],
        out_specs=pl.BlockSpec((None, a, b), lambda e: (e, 0, 0)),
        out_shape=jax.ShapeDtypeStruct((n_exp, a, b), dtype),
        compiler_params=_cparams(("parallel",)),
        name="cast_weights",
    )(w4)


def _expert_kernel(te_ref, ts_ref, nv_ref, xs_ref, wg_ref, wu_ref, wd_ref, ys_ref):
    @pl.when(pl.program_id(0) < nv_ref[0])
    def _():
        x_lo, x_hi = _unpack_halves(xs_ref[...])
        x_lo, x_hi = x_lo.astype(BF16), x_hi.astype(BF16)
        half = x_lo.shape[1]

        def proj(w_ref):
            return (jnp.dot(x_lo, w_ref[:half, :], preferred_element_type=F32)
                    + jnp.dot(x_hi, w_ref[half:, :], preferred_element_type=F32))

        g = proj(wg_ref)
        u = proj(wu_ref)
        hmid = (g * jax.nn.sigmoid(g) * u).astype(BF16)
        ys_ref[...] = _pack_halves(jnp.dot(hmid, wd_ref[...], preferred_element_type=F32))


def expert_mlp(xs, w_gate, w_up, w_down, tile_expert, tile_src, n_valid, tm):
    n_slots, half = xs.shape
    _, d, f = w_gate.shape
    assert d == 2 * half
    n_tiles = n_slots // tm
    return pl.pallas_call(
        _expert_kernel,
        grid_spec=pltpu.PrefetchScalarGridSpec(
            num_scalar_prefetch=3,
            grid=(n_tiles,),
            in_specs=[pl.BlockSpec((tm, half), lambda t, te, ts, nv: (ts[t], 0)),
                      pl.BlockSpec((None, d, f), lambda t, te, ts, nv: (te[t], 0, 0)),
                      pl.BlockSpec((None, d, f), lambda t, te, ts, nv: (te[t], 0, 0)),
                      pl.BlockSpec((None, f, d), lambda t, te, ts, nv: (te[t], 0, 0))],
            out_specs=pl.BlockSpec((tm, half), lambda t, te, ts, nv: (ts[t], 0)),
        ),
        out_shape=jax.ShapeDtypeStruct((n_slots, half), jnp.uint32),
        compiler_params=_cparams(("arbitrary",)),
        name="moe_experts",
    )(tile_expert, tile_src, n_valid, xs, w_gate, w_up, w_down)


def _combine_kernel(dest_ref, x_ref, w_ref, g_ref, b_ref, ys_ref, of_ref, ob_ref, buf, sem, *, tt, n_tok, alpha):
    i = pl.program_id(0)
    n_steps = pl.num_programs(0)

    def row_copy(step, slot, kk, r):
        tok = step * tt + r
        return pltpu.make_async_copy(ys_ref.at[pl.ds(dest_ref[kk * n_tok + tok], 1), :],
                                     buf.at[slot, kk, pl.ds(r, 1), :], sem.at[slot])

    def start_tile(step, slot):
        def body(r, c):
            row_copy(step, slot, 0, r).start()
            row_copy(step, slot, 1, r).start()
            return c
        lax.fori_loop(0, tt, body, 0, unroll=ROW_DMA_UNROLL)

    def wait_tile(step, slot):
        def body(r, c):
            row_copy(step, slot, 0, r).wait()
            row_copy(step, slot, 1, r).wait()
            return c
        lax.fori_loop(0, tt, body, 0, unroll=ROW_DMA_UNROLL)

    slot = i % 2

    @pl.when(i == 0)
    def _():
        start_tile(0, 0)

    @pl.when(i + 1 < n_steps)
    def _():
        start_tile(i + 1, 1 - slot)

    wait_tile(i, slot)
    w = w_ref[...]
    lo1, hi1 = _unpack_halves(buf[slot, 0])
    lo2, hi2 = _unpack_halves(buf[slot, 1])
    y = jnp.concatenate([w[:, 0:1] * lo1 + w[:, 1:2] * lo2, w[:, 0:1] * hi1 + w[:, 1:2] * hi2], axis=1)
    out = _layer_norm_rows(alpha * x_ref[...] + y, g_ref[...], b_ref[...])
    of_ref[...] = out
    ob_ref[...] = out.astype(BF16)


def combine_layer_norm(x, ys, dest, w_tok, g, b, alpha, tt=256):
    s, d = x.shape
    tt = _tile(s, tt)
    kern = functools.partial(_combine_kernel, tt=tt, n_tok=s, alpha=alpha)
    row = lambda i, dr: (i, 0)
    fixed = lambda i, dr: (0, 0)
    return pl.pallas_call(
        kern,
        grid_spec=pltpu.PrefetchScalarGridSpec(
            num_scalar_prefetch=1,
            grid=(s // tt,),
            in_specs=[pl.BlockSpec((tt, d), row),
                      pl.BlockSpec((tt, TOP_K), row),
                      pl.BlockSpec((1, d), fixed),
                      pl.BlockSpec((1, d), fixed),
                      pl.BlockSpec(memory_space=pl.ANY)],
            out_specs=[pl.BlockSpec((tt, d), row), pl.BlockSpec((tt, d), row)],
            scratch_shapes=[pltpu.VMEM((2, TOP_K, tt, d // 2), jnp.uint32), pltpu.SemaphoreType.DMA((2,))],
        ),
        out_shape=[jax.ShapeDtypeStruct((s, d), F32), jax.ShapeDtypeStruct((s, d), BF16)],
        compiler_params=_cparams(("arbitrary",)),
        name="moe_combine_layer_norm",
    )(dest, x, w_tok, g.reshape(1, d), b.reshape(1, d), ys)


def moe_block(x_f32, w_router, b_router, w_gate, w_up, w_down, ln_g, ln_b, alpha, tm=256):
    s, d = x_f32.shape
    n_exp = w_router.shape[1]
    tm = _tile(s, tm)
    route_i, route_w, counts = router(x_f32, w_router, b_router)
    counts = counts[:, 0]
    padded = ((counts + tm - 1) // tm) * tm
    ends = jnp.cumsum(padded)
    offsets = ends - padded
    e1, e2, r1, r2 = route_i[0], route_i[1], route_i[2], route_i[3]
    dest = jnp.concatenate([offsets[e1] + r1, offsets[e2] + r2]).astype(jnp.int32)
    pad_bounds = jnp.concatenate([offsets + counts, ends]).astype(jnp.int32)
    n_slots = TOP_K * s + n_exp * tm
    n_tiles = n_slots // tm
    n_valid = (ends[-1] // tm).astype(jnp.int32)
    tile_start = jnp.arange(n_tiles, dtype=jnp.int32) * tm
    tile_expert = jnp.minimum(jnp.sum(tile_start[:, None] >= ends[None, :], axis=1), n_exp - 1).astype(jnp.int32)
    tile_src = jnp.minimum(jnp.arange(n_tiles, dtype=jnp.int32), n_valid - 1)
    tile_expert = tile_expert[tile_src]

    xs = dispatch(x_f32, dest, pad_bounds, n_slots, n_exp)
    ys = expert_mlp(xs, w_gate, w_up, w_down, tile_expert, tile_src, n_valid.reshape(1), tm)
    w_tok = jnp.stack([route_w[0], route_w[1]], axis=1)
    return combine_layer_norm(x_f32, ys, dest, w_tok, ln_g, ln_b, alpha)


def kernel(x, attn_w_in, attn_lambda, attn_norm_g, attn_w_out, mlstm_w_in, mlstm_gate_b, mlstm_conv_w,
           mlstm_norm_g, mlstm_w_out, ln_mix_g, ln_mix_b, ln_ffn_g, ln_ffn_b, w_router, b_router,
           moe_w_gate, moe_w_up, moe_w_down):
    batch, s, d = x.shape
    assert batch == 1
    depth = ln_mix_g.shape[0]
    alpha = (2 * depth) ** 0.25
    dh = attn_lambda.shape[-1]
    da_heads = d // (2 * dh)
    ml_heads = mlstm_gate_b.shape[-1] // 2
    qk_width = mlstm_conv_w.shape[-1] // 2
    dk = qk_width // ml_heads
    dv = d // ml_heads

    mlstm_wt = jnp.swapaxes(mlstm_w_in, 1, 2)
    xf = x.reshape(s, d)
    xb = xf.astype(BF16)
    for i in range(depth):
        j = i // 2
        if i % 2 == 0:
            lambda_init = 0.8 - 0.6 * math.exp(-0.3 * i)
            lv = attn_lambda[j].astype(F32)
            lam = jnp.exp(jnp.sum(lv[0] * lv[1])) - jnp.exp(jnp.sum(lv[2] * lv[3])) + lambda_init
            proj = matmul(xb, attn_w_in, j, 0, 3 * d, BF16)
            o = diff_attention_core(proj, lam, attn_norm_g[j], lambda_init, da_heads, dh)
            mix = matmul(o, attn_w_out, j, 0, d, BF16)
        else:
            qk_pre = matmul(xb, mlstm_wt, j, 0, 2 * qk_width, F32, w_transposed=True)
            vo = matmul(xb, mlstm_wt, j, 2 * qk_width, 2 * d, BF16, w_transposed=True)
            gates = mlstm_gates(xb, mlstm_wt, j, 2 * qk_width + 2 * d, mlstm_gate_b[j], ml_heads)
            qk = causal_conv_silu(qk_pre, mlstm_conv_w[j], qk_width, dk ** -0.5)
            hcore = mlstm_core(qk, vo, gates.T, mlstm_norm_g[j], ml_heads, dk, dv)
            mix = matmul(hcore, mlstm_w_out, j, 0, d, BF16)
        xf, xb = residual_layer_norm(xf, mix, ln_mix_g[i], ln_mix_b[i], alpha)
        xf, xb = moe_block(xf, w_router, b_router, cast_layer_weights(moe_w_gate, i), cast_layer_weights(moe_w_up, i),
                           cast_layer_weights(moe_w_down, i), ln_ffn_g[i], ln_ffn_b[i], alpha)
    return xf.reshape(batch, s, d)
```

```python
import functools
import math

import jax
import jax.numpy as jnp
from jax import lax
from jax.experimental import pallas as pl
from jax.experimental.pallas import tpu as pltpu

F32 = jnp.float32
BF16 = jnp.bfloat16

LN_EPS = 1e-5
RMS_EPS = 1e-6
N_GROUPS = 4
TOP_K = 2
LOG2E = math.log2(math.e)

V7X_VMEM_BYTES = 64 * 1024 * 1024
VMEM_LIMIT = V7X_VMEM_BYTES - 8 * 1024 * 1024
LANE = 128


def _cparams(sem):
    return pltpu.CompilerParams(dimension_semantics=sem, vmem_limit_bytes=VMEM_LIMIT)


def _tile(n, pref):
    t = min(n, pref)
    while n % t:
        t //= 2
    return t


NT_DIMS = (((1,), (1,)), ((), ()))


def _mm_kernel(a_ref, w_ref, o_ref, *, w_transposed, out_transposed):
    a = a_ref[...].astype(BF16)
    w = w_ref[...].astype(BF16)
    if w_transposed:
        acc = lax.dot_general(a, w, NT_DIMS, preferred_element_type=F32)
    else:
        acc = jnp.dot(a, w, preferred_element_type=F32)
    out = acc.astype(o_ref.dtype)
    o_ref[...] = out.T if out_transposed else out


def matmul(a, w3, layer, col0, n, out_dtype, tm=1024, tn=512, w_transposed=False, out_transposed=False):
    m, k = a.shape
    tm = _tile(m, tm)
    tn = _tile(n, tn)
    assert col0 % tn == 0
    cb = col0 // tn
    if w_transposed:
        w_spec = pl.BlockSpec((None, tn, k), lambda i, j: (layer, cb + j, 0))
    else:
        w_spec = pl.BlockSpec((None, k, tn), lambda i, j: (layer, 0, cb + j))
    if out_transposed:
        out_spec, out_shape = pl.BlockSpec((tn, tm), lambda i, j: (j, i)), (n, m)
    else:
        out_spec, out_shape = pl.BlockSpec((tm, tn), lambda i, j: (i, j)), (m, n)
    return pl.pallas_call(
        functools.partial(_mm_kernel, w_transposed=w_transposed, out_transposed=out_transposed),
        grid=(m // tm, n // tn),
        in_specs=[pl.BlockSpec((tm, k), lambda i, j: (i, 0)), w_spec],
        out_specs=out_spec,
        out_shape=jax.ShapeDtypeStruct(out_shape, out_dtype),
        compiler_params=_cparams(("parallel", "parallel")),
        name="proj_matmul",
    )(a, w3)


def _layer_norm_rows(z, g, b):
    mu = jnp.mean(z, axis=-1, keepdims=True)
    zc = z - mu
    var = jnp.mean(zc * zc, axis=-1, keepdims=True)
    return zc * lax.rsqrt(var + LN_EPS) * g + b


def _ln_kernel(x_ref, y_ref, g_ref, b_ref, of_ref, ob_ref, *, alpha):
    z = alpha * x_ref[...] + y_ref[...].astype(F32)
    out = _layer_norm_rows(z, g_ref[...], b_ref[...])
    of_ref[...] = out
    ob_ref[...] = out.astype(BF16)


def residual_layer_norm(x, y, g, b, alpha, tr=256):
    s, d = x.shape
    tr = _tile(s, tr)
    row = pl.BlockSpec((tr, d), lambda i: (i, 0))
    vec = pl.BlockSpec((1, d), lambda i: (0, 0))
    return pl.pallas_call(
        functools.partial(_ln_kernel, alpha=alpha),
        grid=(s // tr,),
        in_specs=[row, row, vec, vec],
        out_specs=[row, row],
        out_shape=[jax.ShapeDtypeStruct((s, d), F32), jax.ShapeDtypeStruct((s, d), BF16)],
        compiler_params=_cparams(("parallel",)),
        name="residual_layer_norm",
    )(x, y, g.reshape(1, d), b.reshape(1, d))


def _attn_kernel(lam_ref, q_ref, k_ref, vt_ref, g_ref, o_ref, m_sc, l_sc, acc_sc, sa_sc, sb_sc, *,
                 t, dh, q_scale, out_scale):
    n = pl.program_id(1)
    nt_dims = (((1,), (1,)), ((), ()))
    qb = (q_ref[...].astype(F32) * q_scale).astype(BF16)
    m_sc[...] = jnp.full(m_sc.shape, -jnp.inf, F32)
    l_sc[...] = jnp.zeros(l_sc.shape, F32)
    acc_sc[...] = jnp.zeros(acc_sc.shape, F32)

    def scores(j, dst):
        off = pl.multiple_of(j * t, t)
        kb = k_ref[pl.ds(off, t), :]
        for c in range(2):
            dst[c] = lax.dot_general(kb[:, c * dh:(c + 1) * dh], qb[:, c * dh:(c + 1) * dh], nt_dims,
                                     preferred_element_type=F32)

    def accumulate(j, src, masked):
        off = pl.multiple_of(j * t, t)
        vtb = vt_ref[:, pl.ds(off, t)]
        for c in range(2):
            s = src[c]
            if masked:
                krow = lax.broadcasted_iota(jnp.int32, (t, t), 0)
                qcol = lax.broadcasted_iota(jnp.int32, (t, t), 1)
                s = jnp.where(krow <= qcol, s, -jnp.inf)
            m_prev = m_sc[c]
            m_new = jnp.maximum(m_prev, jnp.max(s, axis=0, keepdims=True))
            a = jnp.exp2(m_prev - m_new)
            p = jnp.exp2(s - m_new)
            l_sc[c] = a * l_sc[c] + jnp.sum(p, axis=0, keepdims=True)
            acc_sc[c] = a * acc_sc[c] + jnp.dot(vtb, p.astype(BF16), preferred_element_type=F32)
            m_sc[c] = m_new

    scores(0, sa_sc)

    def pair(pp, carry):
        j = 2 * pp
        scores(j + 1, sb_sc)
        accumulate(j, sa_sc, False)
        scores(j + 2, sa_sc)
        accumulate(j + 1, sb_sc, False)
        return carry

    lax.fori_loop(0, n // 2, pair, 0)

    @pl.when(n % 2 == 0)
    def _():
        accumulate(n, sa_sc, True)

    @pl.when(n % 2 == 1)
    def _():
        scores(n, sb_sc)
        accumulate(n - 1, sa_sc, False)
        accumulate(n, sb_sc, True)

    lam = lam_ref[0, 0]
    o = acc_sc[0] / l_sc[0] - lam * (acc_sc[1] / l_sc[1])
    o = o * lax.rsqrt(jnp.mean(o * o, axis=0, keepdims=True) + RMS_EPS)
    o = o * (g_ref[...] * out_scale)
    o_ref[...] = o.T.astype(o_ref.dtype)


def diff_attention_core(qk, vt, lam, norm_g, lambda_init, n_heads, dh, t=1024):
    s = qk.shape[0]
    hd = 2 * dh
    d = n_heads * hd
    t = _tile(s, t)
    kern = functools.partial(_attn_kernel, t=t, dh=dh, q_scale=(dh ** -0.5) * LOG2E,
                             out_scale=1.0 - lambda_init)
    per_head = dict(pipeline_mode=pl.Buffered(1))
    return pl.pallas_call(
        kern,
        grid=(n_heads, s // t),
        in_specs=[pl.BlockSpec(memory_space=pltpu.SMEM),
                  pl.BlockSpec((t, hd), lambda h, i: (i, h)),
                  pl.BlockSpec((s, hd), lambda h, i: (0, n_heads + h), **per_head),
                  pl.BlockSpec((hd, s), lambda h, i: (h, 0), **per_head),
                  pl.BlockSpec((hd, 1), lambda h, i: (0, 0))],
        out_specs=pl.BlockSpec((t, hd), lambda h, i: (i, h)),
        out_shape=jax.ShapeDtypeStruct((s, d), BF16),
        scratch_shapes=[pltpu.VMEM((2, 1, t), F32), pltpu.VMEM((2, 1, t), F32), pltpu.VMEM((2, hd, t), F32),
                        pltpu.VMEM((2, t, t), F32), pltpu.VMEM((2, t, t), F32)],
        compiler_params=_cparams(("parallel", "arbitrary")),
        name="diff_attention",
    )(lam.reshape(1, 1), qk, qk, vt, norm_g.reshape(hd, 1))


def _gates_kernel(a_ref, w_ref, b_ref, o_ref, *, n_heads):
    z = lax.dot_general(a_ref[...], w_ref[...].astype(BF16), NT_DIMS, preferred_element_type=F32) + b_ref[...]
    log_sig = jnp.minimum(z, 0.0) - jnp.log(1.0 + jnp.exp(-jnp.abs(z)))
    col = lax.broadcasted_iota(jnp.int32, z.shape, 1)
    o_ref[...] = jnp.where(col >= n_heads, log_sig, z)


def mlstm_gates(a, wt3, layer, row0, bias, n_heads, tm=1024):
    m, k = a.shape
    n = 2 * n_heads
    assert row0 % n == 0
    rb = row0 // n
    tm = _tile(m, tm)
    return pl.pallas_call(
        functools.partial(_gates_kernel, n_heads=n_heads),
        grid=(m // tm,),
        in_specs=[pl.BlockSpec((tm, k), lambda i: (i, 0)),
                  pl.BlockSpec((None, n, k), lambda i: (layer, rb, 0)),
                  pl.BlockSpec((1, n), lambda i: (0, 0))],
        out_specs=pl.BlockSpec((tm, n), lambda i: (i, 0)),
        out_shape=jax.ShapeDtypeStruct((m, n), F32),
        compiler_params=_cparams(("parallel",)),
        name="mlstm_gates",
    )(a, wt3, bias.reshape(1, n))


def _conv_kernel(u_ref, w_ref, o_ref, tail_sc, *, n_taps, q_tiles, q_scale):
    j = pl.program_id(0)
    ti = pl.program_id(1)

    @pl.when(ti == 0)
    def _():
        tail_sc[...] = jnp.zeros(tail_sc.shape, F32)

    u = u_ref[...]
    tr = u.shape[0]
    tail = tail_sc[...]
    row8 = lax.broadcasted_iota(jnp.int32, tail.shape, 0)
    acc = u * w_ref[n_taps - 1:n_taps, :]
    for sh in range(1, n_taps):
        rolled = pltpu.roll(u, sh, 0)
        head = jnp.where(row8 < sh, pltpu.roll(tail, sh, 0), rolled[:8])
        shifted = jnp.concatenate([head, rolled[8:]], axis=0)
        acc = acc + shifted * w_ref[n_taps - 1 - sh:n_taps - sh, :]
    tail_sc[...] = u[tr - 8:, :]
    y = acc * jax.nn.sigmoid(acc)
    y = y * jnp.where(j < q_tiles, q_scale, 1.0)
    o_ref[...] = y.astype(o_ref.dtype)


def causal_conv_silu(u, w, qk_width, q_scale, tr=512, tc=512):
    s, c = u.shape
    n_taps = w.shape[0]
    tr = _tile(s, tr)
    tc = _tile(qk_width, tc)
    kern = functools.partial(_conv_kernel, n_taps=n_taps, q_tiles=qk_width // tc, q_scale=q_scale)
    return pl.pallas_call(
        kern,
        grid=(c // tc, s // tr),
        in_specs=[pl.BlockSpec((tr, tc), lambda j, i: (i, j)),
                  pl.BlockSpec((n_taps, tc), lambda j, i: (0, j))],
        out_specs=pl.BlockSpec((tr, tc), lambda j, i: (i, j)),
        out_shape=jax.ShapeDtypeStruct((s, c), BF16),
        scratch_shapes=[pltpu.VMEM((8, tc), F32)],
        compiler_params=_cparams(("parallel", "arbitrary")),
        name="causal_conv_silu",
    )(u, w)


def _mlstm_kernel(q_ref, k_ref, v_ref, og_ref, gt_ref, ng_ref, o_ref, c_sc, m_sc, *, lc, dk, dv, n_heads, hps):
    hg = pl.program_id(0)
    ci = pl.program_id(1)

    @pl.when(ci == 0)
    def _():
        c_sc[...] = jnp.zeros(c_sc.shape, F32)
        m_sc[...] = jnp.zeros(m_sc.shape, F32)

    lane8 = lax.broadcasted_iota(jnp.int32, (8, lc), 1)
    half = lc // 2
    rows = lax.broadcasted_iota(jnp.int32, (lc, lc), 0)
    cols = lax.broadcasted_iota(jnp.int32, (lc, lc), 1)
    ones_col = (lax.broadcasted_iota(jnp.int32, (lc, LANE), 1) == 0).astype(BF16)

    for hh in range(hps):
        h = hg * hps + hh
        ig_row = gt_ref[pl.ds(h, 1), :]
        lf_row = gt_ref[pl.ds(h + n_heads, 1), :]

        b8 = jnp.broadcast_to(lf_row, (8, lc))
        d = 1
        while d < lc:
            b8 = b8 + jnp.where(lane8 >= d, pltpu.roll(b8, d, 1), 0.0)
            d *= 2
        b_row = b8[0:1, :]

        stacked = jnp.where(rows < half, jnp.broadcast_to(b_row, (lc, lc)), jnp.broadcast_to(ig_row, (lc, lc)))
        stacked_t = stacked.T
        b_col = stacked_t[:, 0:1]
        ig_col = stacked_t[:, half:half + 1]

        m_prev = m_sc[hh]
        dlog = jnp.where(cols <= rows, b_col - b_row + ig_row, -jnp.inf)
        inter = b_col + m_prev
        m_out = jnp.maximum(inter, jnp.max(dlog, axis=-1, keepdims=True))
        dw = jnp.exp(dlog - m_out)
        inter_w = jnp.exp(inter - m_out)

        q = q_ref[:, hh * dk:(hh + 1) * dk]
        k = k_ref[:, hh * dk:(hh + 1) * dk]
        v_ext = jnp.concatenate([v_ref[:, hh * dv:(hh + 1) * dv], ones_col], axis=1)

        scores = lax.dot_general(q, k, NT_DIMS, preferred_element_type=F32) * dw
        c_ext = c_sc[hh]
        num_ext = inter_w * jnp.dot(q, c_ext.astype(BF16), preferred_element_type=F32) \
            + jnp.dot(scores.astype(BF16), v_ext, preferred_element_type=F32)
        num = num_ext[:, :dv]
        den = num_ext[:, dv:dv + 1]
        hid = num / jnp.maximum(jnp.abs(den), jnp.exp(-m_out))

        b_last = b_row[:, lc - 1:lc]
        g_col = b_last - b_col + ig_col
        m_new = jnp.maximum(b_last + m_prev, jnp.max(g_col, axis=0, keepdims=True))
        decay = jnp.exp(b_last + m_prev - m_new)
        wg_col = jnp.exp(g_col - m_new)
        vw_ext = (v_ext.astype(F32) * wg_col).astype(BF16)
        k_t = k.astype(F32).T.astype(BF16)
        c_sc[hh] = decay * c_ext + jnp.dot(k_t, vw_ext, preferred_element_type=F32)
        m_sc[hh] = m_new

        hn = hid * lax.rsqrt(jnp.mean(hid * hid, axis=-1, keepdims=True) + RMS_EPS)
        og = og_ref[:, hh * dv:(hh + 1) * dv].astype(F32)
        o_ref[:, hh * dv:(hh + 1) * dv] = (hn * ng_ref[:, hh * dv:(hh + 1) * dv] * jax.nn.sigmoid(og)).astype(o_ref.dtype)


def mlstm_core(qk, vo, gates_t, norm_g, n_heads, dk, dv, lc=256, hps=4):
    s = qk.shape[0]
    d = n_heads * dv
    lc = _tile(s, lc)
    hps = math.gcd(n_heads, hps)
    n_groups = n_heads // hps
    kern = functools.partial(_mlstm_kernel, lc=lc, dk=dk, dv=dv, n_heads=n_heads, hps=hps)
    return pl.pallas_call(
        kern,
        grid=(n_groups, s // lc),
        in_specs=[pl.BlockSpec((lc, hps * dk), lambda g, c: (c, g)),
                  pl.BlockSpec((lc, hps * dk), lambda g, c: (c, n_groups + g)),
                  pl.BlockSpec((lc, hps * dv), lambda g, c: (c, g)),
                  pl.BlockSpec((lc, hps * dv), lambda g, c: (c, n_groups + g)),
                  pl.BlockSpec((2 * n_heads, lc), lambda g, c: (0, c)),
                  pl.BlockSpec((1, hps * dv), lambda g, c: (0, g))],
        out_specs=pl.BlockSpec((lc, hps * dv), lambda g, c: (c, g)),
        out_shape=jax.ShapeDtypeStruct((s, d), BF16),
        scratch_shapes=[pltpu.VMEM((hps, dk, dv + LANE), F32), pltpu.VMEM((hps, 1, 1), F32)],
        compiler_params=_cparams(("parallel", "arbitrary")),
        name="mlstm_chunks",
    )(qk, qk, vo, vo, gates_t, norm_g.reshape(1, d))


def _split_bf16(x):
    hi = x.astype(BF16)
    lo = (x - hi.astype(F32)).astype(BF16)
    return hi, lo


def _router_kernel(x_ref, wt_ref, br_ref, ri_ref, rw_ref, cnt_ref, lg_sc, base_sc, *, n_exp, tm):
    step = pl.program_id(0)
    per_group = n_exp // N_GROUPS

    @pl.when(step == 0)
    def _():
        base_sc[...] = jnp.zeros(base_sc.shape, F32)

    nt = (((1,), (1,)), ((), ()))
    x_hi, x_lo = _split_bf16(x_ref[...])
    w_hi, w_lo = _split_bf16(wt_ref[...])
    lg_sc[...] = (lax.dot_general(w_hi, x_hi, nt, preferred_element_type=F32)
                  + lax.dot_general(w_hi, x_lo, nt, preferred_element_type=F32)
                  + lax.dot_general(w_lo, x_hi, nt, preferred_element_type=F32))

    logit = [lg_sc[e:e + 1, :] for e in range(n_exp)]
    mx = functools.reduce(jnp.maximum, logit)
    ex = [jnp.exp(l - mx) for l in logit]
    inv = 1.0 / functools.reduce(lambda a, b: a + b, ex)
    prob = [e * inv for e in ex]
    sel = [prob[e] + br_ref[e:e + 1, :] for e in range(n_exp)]

    def top2_sum(a):
        hi01, lo01 = jnp.maximum(a[0], a[1]), jnp.minimum(a[0], a[1])
        hi23, lo23 = jnp.maximum(a[2], a[3]), jnp.minimum(a[2], a[3])
        return jnp.maximum(hi01, hi23) + jnp.maximum(jnp.minimum(hi01, hi23), jnp.maximum(lo01, lo23))

    assert per_group == 4
    gscore = [top2_sum(sel[g * per_group:(g + 1) * per_group]) for g in range(N_GROUPS)]
    best, grp = gscore[0], jnp.zeros_like(gscore[0], dtype=jnp.int32)
    for g in range(1, N_GROUPS):
        upd = gscore[g] > best
        grp = jnp.where(upd, g, grp)
        best = jnp.where(upd, gscore[g], best)

    def pick(vals, i):
        out = vals[i]
        for g in range(1, N_GROUPS):
            out = jnp.where(grp == g, vals[g * per_group + i], out)
        return out

    gs = [pick(sel, i) for i in range(per_group)]
    gp = [pick(prob, i) for i in range(per_group)]
    v1, i1, p1 = gs[0], jnp.zeros_like(grp), gp[0]
    for i in range(1, per_group):
        upd = gs[i] > v1
        v1, i1, p1 = jnp.where(upd, gs[i], v1), jnp.where(upd, i, i1), jnp.where(upd, gp[i], p1)
    v2 = jnp.full_like(v1, -jnp.inf)
    i2, p2 = jnp.zeros_like(grp), jnp.zeros_like(p1)
    for i in range(per_group):
        upd = (i1 != i) & (gs[i] > v2)
        v2, i2, p2 = jnp.where(upd, gs[i], v2), jnp.where(upd, i, i2), jnp.where(upd, gp[i], p2)
    e1 = grp * per_group + i1
    e2 = grp * per_group + i2
    wsum = p1 + p2
    rw_ref[...] = jnp.zeros(rw_ref.shape, F32)
    rw_ref[0:1, :] = p1 / wsum
    rw_ref[1:2, :] = p2 / wsum

    for e in range(n_exp):
        lg_sc[e:e + 1, :] = ((e1 == e) | (e2 == e)).astype(F32)
    mask = lg_sc[...]
    r_i = lax.broadcasted_iota(jnp.int32, (tm, tm), 0)
    c_i = lax.broadcasted_iota(jnp.int32, (tm, tm), 1)
    upper = (r_i < c_i).astype(BF16)
    lg_sc[...] = jnp.dot(mask.astype(BF16), upper, preferred_element_type=F32) + base_sc[...]
    new_base = base_sc[...] + jnp.sum(mask, axis=1, keepdims=True)
    base_sc[...] = new_base
    cnt_ref[...] = jnp.broadcast_to(new_base, cnt_ref.shape).astype(jnp.int32)
    r1 = jnp.zeros_like(p1)
    r2 = jnp.zeros_like(p1)
    for e in range(n_exp):
        excl_e = lg_sc[e:e + 1, :]
        r1 = jnp.where(e1 == e, excl_e, r1)
        r2 = jnp.where(e2 == e, excl_e, r2)
    ri_ref[...] = jnp.zeros(ri_ref.shape, jnp.int32)
    ri_ref[0:1, :] = e1
    ri_ref[1:2, :] = e2
    ri_ref[2:3, :] = r1.astype(jnp.int32)
    ri_ref[3:4, :] = r2.astype(jnp.int32)


def router(x, w_router, b_router, tm=512):
    s, d = x.shape
    n_exp = w_router.shape[1]
    tm = _tile(s, tm)
    kern = functools.partial(_router_kernel, n_exp=n_exp, tm=tm)
    return pl.pallas_call(
        kern,
        grid=(s // tm,),
        in_specs=[pl.BlockSpec((tm, d), lambda i: (i, 0)),
                  pl.BlockSpec((n_exp, d), lambda i: (0, 0)),
                  pl.BlockSpec((n_exp, 1), lambda i: (0, 0))],
        out_specs=[pl.BlockSpec((8, tm), lambda i: (0, i)),
                   pl.BlockSpec((8, tm), lambda i: (0, i)),
                   pl.BlockSpec((n_exp, LANE), lambda i: (0, 0))],
        out_shape=[jax.ShapeDtypeStruct((8, s), jnp.int32), jax.ShapeDtypeStruct((8, s), F32),
                   jax.ShapeDtypeStruct((n_exp, LANE), jnp.int32)],
        scratch_shapes=[pltpu.VMEM((n_exp, tm), F32), pltpu.VMEM((n_exp, 1), F32)],
        compiler_params=_cparams(("arbitrary",)),
        name="moe_router",
    )(x, w_router.T, b_router.reshape(n_exp, 1))


def _pack_halves(x):
    half = x.shape[1] // 2
    lo = lax.bitcast_convert_type(x[:, :half].astype(BF16).astype(F32), jnp.uint32) >> 16
    hi = lax.bitcast_convert_type(x[:, half:].astype(BF16).astype(F32), jnp.uint32) & jnp.uint32(0xFFFF0000)
    return lo | hi


def _unpack_halves(w):
    lo = lax.bitcast_convert_type(w << 16, F32)
    hi = lax.bitcast_convert_type(w & jnp.uint32(0xFFFF0000), F32)
    return lo, hi


ROW_DMA_UNROLL = 8


def _dispatch_kernel(dest_ref, pad_ref, x_ref, xs_ref, pk_sc, zero_sc, sem, *, tt, n_tok, n_exp):
    i = pl.program_id(0)
    base = i * tt
    pk_sc[...] = _pack_halves(x_ref[...])

    def row_copy(src, r_src, r_dst):
        return pltpu.make_async_copy(src.at[pl.ds(r_src, 1), :], xs_ref.at[pl.ds(r_dst, 1), :], sem)

    @pl.when(i == 0)
    def _():
        zero_sc[...] = jnp.zeros(zero_sc.shape, zero_sc.dtype)
        for e in range(n_exp):
            lo, hi = pad_ref[e], pad_ref[n_exp + e]
            lax.fori_loop(lo, hi, lambda r, c: (row_copy(zero_sc, 0, r).start(), c)[1], 0)
        for e in range(n_exp):
            lo, hi = pad_ref[e], pad_ref[n_exp + e]
            lax.fori_loop(lo, hi, lambda r, c: (row_copy(zero_sc, 0, r).wait(), c)[1], 0)

    def start_tok(r, c):
        row_copy(pk_sc, r, dest_ref[base + r]).start()
        row_copy(pk_sc, r, dest_ref[n_tok + base + r]).start()
        return c

    def wait_tok(r, c):
        row_copy(pk_sc, r, dest_ref[base + r]).wait()
        row_copy(pk_sc, r, dest_ref[n_tok + base + r]).wait()
        return c

    lax.fori_loop(0, tt, start_tok, 0, unroll=ROW_DMA_UNROLL)
    lax.fori_loop(0, tt, wait_tok, 0, unroll=ROW_DMA_UNROLL)


def dispatch(x, dest, pad_bounds, n_slots, n_exp, tt=256):
    s, d = x.shape
    tt = _tile(s, tt)
    kern = functools.partial(_dispatch_kernel, tt=tt, n_tok=s, n_exp=n_exp)
    return pl.pallas_call(
        kern,
        grid_spec=pltpu.PrefetchScalarGridSpec(
            num_scalar_prefetch=2,
            grid=(s // tt,),
            in_specs=[pl.BlockSpec((tt, d), lambda i, dr, pr: (i, 0))],
            out_specs=pl.BlockSpec(memory_space=pl.ANY),
            scratch_shapes=[pltpu.VMEM((tt, d // 2), jnp.uint32), pltpu.VMEM((8, d // 2), jnp.uint32),
                            pltpu.SemaphoreType.DMA],
        ),
        out_shape=jax.ShapeDtypeStruct((n_slots, d // 2), jnp.uint32),
        compiler_params=_cparams(("arbitrary",)),
        name="moe_dispatch",
    )(dest, pad_bounds, x)


def _cast_kernel(x_ref, o_ref):
    o_ref[...] = x_ref[...].astype(o_ref.dtype)


def cast_layer_weights(w4, layer, dtype=BF16):
    _, n_exp, a, b = w4.shape
    return pl.pallas_call(
        _cast_kernel,
        grid=(n_exp,),
        in_specs=[pl.BlockSpec((None, None, a, b), lambda e: (layer, e, 0, 0))],
        out_specs=pl.BlockSpec((None, a, b), lambda e: (e, 0, 0)),
        out_shape=jax.ShapeDtypeStruct((n_exp, a, b), dtype),
        compiler_params=_cparams(("parallel",)),
        name="cast_weights",
    )(w4)


def _expert_kernel(te_ref, ts_ref, nv_ref, xs_ref, wg_ref, wu_ref, wd_ref, ys_ref):
    @pl.when(pl.program_id(0) < nv_ref[0])
    def _():
        x_lo, x_hi = _unpack_halves(xs_ref[...])
        x_lo, x_hi = x_lo.astype(BF16), x_hi.astype(BF16)
        half = x_lo.shape[1]

        def proj(w_ref):
            return (jnp.dot(x_lo, w_ref[:half, :], preferred_element_type=F32)
                    + jnp.dot(x_hi, w_ref[half:, :], preferred_element_type=F32))

        g = proj(wg_ref)
        u = proj(wu_ref)
        hmid = (g * jax.nn.sigmoid(g) * u).astype(BF16)
        ys_ref[...] = _pack_halves(jnp.dot(hmid, wd_ref[...], preferred_element_type=F32))


def expert_mlp(xs, w_gate, w_up, w_down, tile_expert, tile_src, n_valid, tm):
    n_slots, half = xs.shape
    _, d, f = w_gate.shape
    assert d == 2 * half
    n_tiles = n_slots // tm
    return pl.pallas_call(
        _expert_kernel,
        grid_spec=pltpu.PrefetchScalarGridSpec(
            num_scalar_prefetch=3,
            grid=(n_tiles,),
            in_specs=[pl.BlockSpec((tm, half), lambda t, te, ts, nv: (ts[t], 0)),
                      pl.BlockSpec((None, d, f), lambda t, te, ts, nv: (te[t], 0, 0)),
                      pl.BlockSpec((None, d, f), lambda t, te, ts, nv: (te[t], 0, 0)),
                      pl.BlockSpec((None, f, d), lambda t, te, ts, nv: (te[t], 0, 0))],
            out_specs=pl.BlockSpec((tm, half), lambda t, te, ts, nv: (ts[t], 0)),
        ),
        out_shape=jax.ShapeDtypeStruct((n_slots, half), jnp.uint32),
        compiler_params=_cparams(("arbitrary",)),
        name="moe_experts",
    )(tile_expert, tile_src, n_valid, xs, w_gate, w_up, w_down)


def _combine_kernel(dest_ref, x_ref, w_ref, g_ref, b_ref, ys_ref, of_ref, ob_ref, buf, sem, *, tt, n_tok, alpha):
    i = pl.program_id(0)
    n_steps = pl.num_programs(0)

    def row_copy(step, slot, kk, r):
        tok = step * tt + r
        return pltpu.make_async_copy(ys_ref.at[pl.ds(dest_ref[kk * n_tok + tok], 1), :],
                                     buf.at[slot, kk, pl.ds(r, 1), :], sem.at[slot])

    def start_tile(step, slot):
        def body(r, c):
            row_copy(step, slot, 0, r).start()
            row_copy(step, slot, 1, r).start()
            return c
        lax.fori_loop(0, tt, body, 0, unroll=ROW_DMA_UNROLL)

    def wait_tile(step, slot):
        def body(r, c):
            row_copy(step, slot, 0, r).wait()
            row_copy(step, slot, 1, r).wait()
            return c
        lax.fori_loop(0, tt, body, 0, unroll=ROW_DMA_UNROLL)

    slot = i % 2

    @pl.when(i == 0)
    def _():
        start_tile(0, 0)

    @pl.when(i + 1 < n_steps)
    def _():
        start_tile(i + 1, 1 - slot)

    wait_tile(i, slot)
    w = w_ref[...]
    lo1, hi1 = _unpack_halves(buf[slot, 0])
    lo2, hi2 = _unpack_halves(buf[slot, 1])
    y = jnp.concatenate([w[:, 0:1] * lo1 + w[:, 1:2] * lo2, w[:, 0:1] * hi1 + w[:, 1:2] * hi2], axis=1)
    out = _layer_norm_rows(alpha * x_ref[...] + y, g_ref[...], b_ref[...])
    of_ref[...] = out
    ob_ref[...] = out.astype(BF16)


def combine_layer_norm(x, ys, dest, w_tok, g, b, alpha, tt=256):
    s, d = x.shape
    tt = _tile(s, tt)
    kern = functools.partial(_combine_kernel, tt=tt, n_tok=s, alpha=alpha)
    row = lambda i, dr: (i, 0)
    fixed = lambda i, dr: (0, 0)
    return pl.pallas_call(
        kern,
        grid_spec=pltpu.PrefetchScalarGridSpec(
            num_scalar_prefetch=1,
            grid=(s // tt,),
            in_specs=[pl.BlockSpec((tt, d), row),
                      pl.BlockSpec((tt, TOP_K), row),
                      pl.BlockSpec((1, d), fixed),
                      pl.BlockSpec((1, d), fixed),
                      pl.BlockSpec(memory_space=pl.ANY)],
            out_specs=[pl.BlockSpec((tt, d), row), pl.BlockSpec((tt, d), row)],
            scratch_shapes=[pltpu.VMEM((2, TOP_K, tt, d // 2), jnp.uint32), pltpu.SemaphoreType.DMA((2,))],
        ),
        out_shape=[jax.ShapeDtypeStruct((s, d), F32), jax.ShapeDtypeStruct((s, d), BF16)],
        compiler_params=_cparams(("arbitrary",)),
        name="moe_combine_layer_norm",
    )(dest, x, w_tok, g.reshape(1, d), b.reshape(1, d), ys)


def moe_block(x_f32, w_router, b_router, w_gate, w_up, w_down, ln_g, ln_b, alpha, tm=256):
    s, d = x_f32.shape
    n_exp = w_router.shape[1]
    tm = _tile(s, tm)
    route_i, route_w, counts = router(x_f32, w_router, b_router)
    counts = counts[:, 0]
    padded = ((counts + tm - 1) // tm) * tm
    ends = jnp.cumsum(padded)
    offsets = ends - padded
    e1, e2, r1, r2 = route_i[0], route_i[1], route_i[2], route_i[3]
    dest = jnp.concatenate([offsets[e1] + r1, offsets[e2] + r2]).astype(jnp.int32)
    pad_bounds = jnp.concatenate([offsets + counts, ends]).astype(jnp.int32)
    n_slots = TOP_K * s + n_exp * tm
    n_tiles = n_slots // tm
    n_valid = (ends[-1] // tm).astype(jnp.int32)
    tile_start = jnp.arange(n_tiles, dtype=jnp.int32) * tm
    tile_expert = jnp.minimum(jnp.sum(tile_start[:, None] >= ends[None, :], axis=1), n_exp - 1).astype(jnp.int32)
    tile_src = jnp.minimum(jnp.arange(n_tiles, dtype=jnp.int32), n_valid - 1)
    tile_expert = tile_expert[tile_src]

    xs = dispatch(x_f32, dest, pad_bounds, n_slots, n_exp)
    ys = expert_mlp(xs, w_gate, w_up, w_down, tile_expert, tile_src, n_valid.reshape(1), tm)
    w_tok = jnp.stack([route_w[0], route_w[1]], axis=1)
    return combine_layer_norm(x_f32, ys, dest, w_tok, ln_g, ln_b, alpha)


def kernel(x, attn_w_in, attn_lambda, attn_norm_g, attn_w_out, mlstm_w_in, mlstm_gate_b, mlstm_conv_w,
           mlstm_norm_g, mlstm_w_out, ln_mix_g, ln_mix_b, ln_ffn_g, ln_ffn_b, w_router, b_router,
           moe_w_gate, moe_w_up, moe_w_down):
    batch, s, d = x.shape
    assert batch == 1
    depth = ln_mix_g.shape[0]
    alpha = (2 * depth) ** 0.25
    dh = attn_lambda.shape[-1]
    da_heads = d // (2 * dh)
    ml_heads = mlstm_gate_b.shape[-1] // 2
    qk_width = mlstm_conv_w.shape[-1] // 2
    dk = qk_width // ml_heads
    dv = d // ml_heads

    mlstm_wt = jnp.swapaxes(mlstm_w_in, 1, 2)
    xf = x.reshape(s, d)
    xb = xf.astype(BF16)
    for i in range(depth):
        j = i // 2
        if i % 2 == 0:
            lambda_init = 0.8 - 0.6 * math.exp(-0.3 * i)
            lv = attn_lambda[j].astype(F32)
            lam = jnp.exp(jnp.sum(lv[0] * lv[1])) - jnp.exp(jnp.sum(lv[2] * lv[3])) + lambda_init
            qk = matmul(xb, attn_w_in, j, 0, 2 * d, BF16)
            vt = matmul(xb, attn_w_in, j, 2 * d, d, BF16, out_transposed=True)
            o = diff_attention_core(qk, vt, lam, attn_norm_g[j], lambda_init, da_heads, dh)
            mix = matmul(o, attn_w_out, j, 0, d, BF16)
        else:
            qk_pre = matmul(xb, mlstm_wt, j, 0, 2 * qk_width, F32, w_transposed=True)
            vo = matmul(xb, mlstm_wt, j, 2 * qk_width, 2 * d, BF16, w_transposed=True)
            gates = mlstm_gates(xb, mlstm_wt, j, 2 * qk_width + 2 * d, mlstm_gate_b[j], ml_heads)
            qk = causal_conv_silu(qk_pre, mlstm_conv_w[j], qk_width, dk ** -0.5)
            hcore = mlstm_core(qk, vo, gates.T, mlstm_norm_g[j], ml_heads, dk, dv)
            mix = matmul(hcore, mlstm_w_out, j, 0, d, BF16)
        xf, xb = residual_layer_norm(xf, mix, ln_mix_g[i], ln_mix_b[i], alpha)
        xf, xb = moe_block(xf, w_router, b_router, cast_layer_weights(moe_w_gate, i), cast_layer_weights(moe_w_up, i),
                           cast_layer_weights(moe_w_down, i), ln_ffn_g[i], ln_ffn_b[i], alpha)
    return xf.reshape(batch, s, d)
```
